```python
import jax, jax.numpy as jnp
from jax import lax
import numpy as np

D_MODEL = 1024
BATCH = 8
SEQ = 2048
DEPTH = 1

CHUNK = 64
N_META = 16
CONV_WIDTH = D_MODEL // 2
CONV_K = 3
SB_HEADS = 8
SB_HEAD_DIM = 64
SB_WIDTH = SB_HEADS * SB_HEAD_DIM
Q_BLOCK = 128
PEER_HEADS = 8
PEER_N_KEYS = 128
PEER_N_EXPERTS = PEER_N_KEYS * PEER_N_KEYS
PEER_TOPK = 16
PEER_KEY_DIM = 256
PEER_HALF = PEER_KEY_DIM // 2
PEER_TOKEN_BLOCK = 128
IN_WIDTH = 3 * CONV_WIDTH + 3 * SB_WIDTH + 2 * D_MODEL
EPS = 1e-6

kernel_name = "hybrid_conv_stickbreak_peer_block"


def rmsnorm(x, g):
    xf = x.astype(jnp.float32)
    y = xf * lax.rsqrt(jnp.mean(xf * xf, axis=-1, keepdims=True) + EPS)
    return (y * g.astype(jnp.float32)).astype(x.dtype)


def causal_dwconv(u, w):
    k_width = w.shape[0]
    seq_len = u.shape[1]
    up = jnp.pad(u, ((0, 0), (k_width - 1, 0), (0, 0)))
    y = up[:, 0:seq_len] * w[0]
    for k in range(1, k_width):
        y = y + up[:, k:k + seq_len] * w[k]
    return y


def stick_breaking_attention(q, k, v):
    seq_len = q.shape[2]
    scale = q.shape[-1] ** -0.5
    outs = []
    for start in range(0, seq_len, Q_BLOCK):
        end = min(start + Q_BLOCK, seq_len)
        qb = q[:, :, start:end]
        kb = k[:, :, :end]
        vb = v[:, :, :end]
        z = jnp.einsum('bhqd,bhkd->bhqk', qb, kb).astype(jnp.float32) * scale
        t_pos = jnp.arange(start, end)[:, None]
        s_pos = jnp.arange(end)[None, :]
        past = s_pos < t_pos
        log_one_minus = jnp.where(past, jax.nn.log_sigmoid(-z), 0.0)
        suffix = lax.cumsum(log_one_minus, axis=3, reverse=True) - log_one_minus
        a = jnp.where(past, jnp.exp(jax.nn.log_sigmoid(z) + suffix), 0.0)
        outs.append(jnp.einsum('bhqk,bhkd->bhqd', a.astype(vb.dtype), vb))
    return jnp.concatenate(outs, axis=2)


def peer_layer(h, w_q, subkeys, u_tab, v_tab):
    bsz, seq_len, d = h.shape
    q = (h @ w_q).reshape(bsz, seq_len, PEER_HEADS, 2, PEER_HALF)
    s = jnp.einsum('blhpd,hpkd->blhpk', q, subkeys).astype(jnp.float32)
    s_top, i_top = lax.top_k(s, PEER_TOPK)
    cand = (s_top[..., 0, :, None] + s_top[..., 1, None, :]).reshape(
        bsz, seq_len, PEER_HEADS, PEER_TOPK * PEER_TOPK)
    score, ci = lax.top_k(cand, PEER_TOPK)
    i1 = jnp.take_along_axis(i_top[..., 0, :], ci // PEER_TOPK, axis=-1)
    i2 = jnp.take_along_axis(i_top[..., 1, :], ci % PEER_TOPK, axis=-1)
    idx = i1 * PEER_N_KEYS + i2
    gate = jax.nn.softmax(score, axis=-1)

    n_tok = bsz * seq_len
    n_sel = PEER_HEADS * PEER_TOPK
    pad = (-n_tok) % PEER_TOKEN_BLOCK
    hf = jnp.pad(h.reshape(n_tok, d), ((0, pad), (0, 0)))
    idf = jnp.pad(idx.reshape(n_tok, n_sel), ((0, pad), (0, 0)))
    gf = jnp.pad(gate.reshape(n_tok, n_sel).astype(h.dtype), ((0, pad), (0, 0)))
    n_blk = (n_tok + pad) // PEER_TOKEN_BLOCK
    hf = hf.reshape(n_blk, PEER_TOKEN_BLOCK, d)
    idf = idf.reshape(n_blk, PEER_TOKEN_BLOCK, n_sel)
    gf = gf.reshape(n_blk, PEER_TOKEN_BLOCK, n_sel)

    def expert_block(args):
        hb, ib, gb = args
        u = u_tab[ib]
        act = jax.nn.gelu(jnp.einsum('td,ted->te', hb, u))
        return jnp.einsum('te,ted->td', gb * act, v_tab[ib])

    y = lax.map(expert_block, (hf, idf, gf))
    return y.reshape(n_blk * PEER_TOKEN_BLOCK, d)[:n_tok].reshape(bsz, seq_len, d)


def setup_inputs(seed: int = 0) -> dict:
    key = jax.random.key(seed)
    ks = jax.random.split(key, 14)
    d = D_MODEL
    f32 = jnp.float32
    return {
        "x": jax.random.normal(ks[0], (BATCH, SEQ, d), f32),
        "meta": jax.random.normal(ks[1], (N_META, d), f32),
        "norm1_g": 1.0 + 0.05 * jax.random.normal(ks[2], (d,), f32),
        "w_in": jax.random.normal(ks[3], (d, IN_WIDTH), f32) * d ** -0.5,
        "conv_w": jax.random.normal(ks[4], (CONV_K, CONV_WIDTH), f32) * CONV_K ** -0.5,
        "w_branch_a": jax.random.normal(ks[5], (CONV_WIDTH, d), f32) * CONV_WIDTH ** -0.5,
        "w_branch_b": jax.random.normal(ks[6], (SB_WIDTH, d), f32) * SB_WIDTH ** -0.5,
        "w_out": jax.random.normal(ks[7], (d, d), f32) * d ** -0.5,
        "norm2_g": 1.0 + 0.05 * jax.random.normal(ks[8], (d,), f32),
        "peer_w_q": jax.random.normal(ks[9], (d, PEER_HEADS * PEER_KEY_DIM), f32) * d ** -0.5,
        "peer_subkeys": jax.random.normal(ks[10], (PEER_HEADS, 2, PEER_N_KEYS, PEER_HALF), f32) * PEER_HALF ** -0.5,
        "peer_u": jax.random.normal(ks[11], (PEER_N_EXPERTS, d), f32) * d ** -0.5,
        "peer_v": jax.random.normal(ks[12], (PEER_N_EXPERTS, d), f32) * 0.5,
        "final_g": 1.0 + 0.05 * jax.random.normal(ks[13], (d,), f32),
    }


def reference(x, meta, norm1_g, w_in, conv_w, w_branch_a, w_branch_b, w_out,
              norm2_g, peer_w_q, peer_subkeys, peer_u, peer_v, final_g):
    bsz = x.shape[0]
    meta_b = jnp.broadcast_to(meta.astype(x.dtype)[None], (bsz, N_META, D_MODEL))
    h_res = jnp.concatenate([meta_b, x], axis=1)
    seq_len = h_res.shape[1]

    for _ in range(DEPTH):
        h = rmsnorm(h_res, norm1_g)
        proj = h @ w_in
        c0 = 3 * CONV_WIDTH
        c1 = c0 + 3 * SB_WIDTH
        b_a, c_a, u_a = jnp.split(proj[..., :c0], 3, axis=-1)
        q, k, v = jnp.split(proj[..., c0:c1], 3, axis=-1)
        gate_a, gate_b = jnp.split(proj[..., c1:], 2, axis=-1)

        y_a = (b_a * causal_dwconv(c_a * u_a, conv_w)) @ w_branch_a

        def heads(t):
            return t.reshape(bsz, seq_len, SB_HEADS, SB_HEAD_DIM).transpose(0, 2, 1, 3)
        o = stick_breaking_attention(heads(q), heads(k), heads(v))
        o = o.transpose(0, 2, 1, 3).reshape(bsz, seq_len, SB_WIDTH)
        y_b = o @ w_branch_b

        mixed = jax.nn.sigmoid(gate_a) * y_a + jax.nn.sigmoid(gate_b) * y_b
        h_res = h_res + mixed @ w_out

        h2 = rmsnorm(h_res, norm2_g)
        h_res = h_res + peer_layer(h2, peer_w_q, peer_subkeys, peer_u, peer_v)

    out = rmsnorm(h_res, final_g)
    return out[:, N_META:]
```

```python
import functools

import jax
import jax.numpy as jnp
from jax import lax
from jax.experimental import pallas as pl
from jax.experimental.pallas import tpu as pltpu

D_MODEL = 1024
N_META = 16
CONV_WIDTH = 512
CONV_K = 3
SB_HEADS = 8
SB_HEAD_DIM = 64
SB_WIDTH = SB_HEADS * SB_HEAD_DIM
PEER_HEADS = 8
PEER_N_KEYS = 128
PEER_N_EXPERTS = PEER_N_KEYS * PEER_N_KEYS
PEER_TOPK = 16
PEER_HALF = 128
PEER_Q_WIDTH = PEER_HEADS * 2 * PEER_HALF
N_SEL = PEER_HEADS * PEER_TOPK
GATE_WIDTH = 2 * D_MODEL
EPS = 1e-6

LANES = 128
SUBLANES = 8
VMEM_LIMIT = 56 * 1024 * 1024

F32 = jnp.float32
BF16 = jnp.bfloat16

_NT = (((1,), (1,)), ((), ()))


def _params(*sem):
    return pltpu.CompilerParams(dimension_semantics=sem, vmem_limit_bytes=VMEM_LIMIT)


def _rms(x, g):
    return x * lax.rsqrt(jnp.mean(x * x, axis=-1, keepdims=True) + EPS) * g


def _in_proj_kernel(x_ref, g_ref, w_ref, conv_ref, q_ref, k_ref, v_ref, gate_ref):
    h = _rms(x_ref[...], g_ref[...]).astype(BF16)
    c0 = 3 * CONV_WIDTH
    conv_ref[...] = jnp.dot(h, w_ref[:, :c0], preferred_element_type=F32)
    scale = SB_HEAD_DIM ** -0.5
    q_ref[...] = (jnp.dot(h, w_ref[:, c0:c0 + SB_WIDTH], preferred_element_type=F32) * scale).astype(BF16)
    k_ref[...] = jnp.dot(h, w_ref[:, c0 + SB_WIDTH:c0 + 2 * SB_WIDTH], preferred_element_type=F32).astype(BF16)
    v_ref[...] = jnp.dot(h, w_ref[:, c0 + 2 * SB_WIDTH:c0 + 3 * SB_WIDTH], preferred_element_type=F32).astype(BF16)
    gate_ref[...] = jnp.dot(h, w_ref[:, c0 + 3 * SB_WIDTH:], preferred_element_type=F32)


def _in_proj(x2d, g, w_in_bf, tm):
    n = x2d.shape[0]
    in_width = w_in_bf.shape[1]
    row = lambda width: pl.BlockSpec((tm, width), lambda i: (i, 0))
    return pl.pallas_call(
        _in_proj_kernel,
        grid=(n // tm,),
        in_specs=[row(D_MODEL),
                  pl.BlockSpec((1, D_MODEL), lambda i: (0, 0)),
                  pl.BlockSpec((D_MODEL, in_width), lambda i: (0, 0))],
        out_specs=[row(3 * CONV_WIDTH), row(SB_WIDTH), row(SB_WIDTH), row(SB_WIDTH), row(GATE_WIDTH)],
        out_shape=[jax.ShapeDtypeStruct((n, 3 * CONV_WIDTH), F32),
                   jax.ShapeDtypeStruct((n, SB_WIDTH), BF16),
                   jax.ShapeDtypeStruct((n, SB_WIDTH), BF16),
                   jax.ShapeDtypeStruct((n, SB_WIDTH), BF16),
                   jax.ShapeDtypeStruct((n, GATE_WIDTH), F32)],
        compiler_params=_params("arbitrary"),
    )(x2d, g, w_in_bf)


ATT_TILE = 128


def _attn_kernel(q_ref, k_ref, v_ref, km_ref, vm_ref, o_ref):
    qi = pl.program_id(2)
    t = ATT_TILE
    q2 = q_ref[...]
    lane = lax.broadcasted_iota(jnp.int32, (1, LANES), 1)
    head_lanes = [lane < SB_HEAD_DIM, lane >= SB_HEAD_DIM]
    qh = [jnp.where(m, q2, jnp.zeros_like(q2)) for m in head_lanes]

    def tri(n):
        r = lax.broadcasted_iota(jnp.int32, (n, n), 0)
        c = lax.broadcasted_iota(jnp.int32, (n, n), 1)
        return r, c

    r, c = tri(t)
    suffix_mat = jnp.where(r > c, 1.0, 0.0).astype(BF16)
    past = c < r
    rm, cm = tri(N_META)
    suffix_mat_meta = jnp.where(rm > cm, 1.0, 0.0).astype(BF16)

    def block(qm, k_t, v_t, smat, o, csum, mask):
        z = lax.dot_general(qm, k_t, _NT, preferred_element_type=F32)
        lom = -(jnp.maximum(z, 0.0) + jnp.log1p(jnp.exp(-jnp.abs(z))))
        if mask is not None:
            lom = jnp.where(mask, lom, 0.0)
        hi = lom.astype(BF16)
        lo = (lom - hi.astype(F32)).astype(BF16)
        suf = (jnp.dot(hi, smat, preferred_element_type=F32)
               + jnp.dot(lo, smat, preferred_element_type=F32))
        a = jnp.exp(z + lom + suf + csum)
        if mask is not None:
            a = jnp.where(mask, a, 0.0)
        o = o + jnp.dot(a.astype(BF16), v_t, preferred_element_type=F32)
        csum = csum + jnp.sum(lom, axis=1, keepdims=True)
        return o, csum

    def step(start, carry, mask):
        k_t = k_ref[pl.ds(start, t), :]
        v_t = v_ref[pl.ds(start, t), :]
        out = []
        for h in range(2):
            out.extend(block(qh[h], k_t, v_t, suffix_mat, carry[2 * h], carry[2 * h + 1], mask))
        return tuple(out)

    zero_o = jnp.zeros((t, LANES), F32)
    zero_c = jnp.zeros((t, 1), F32)
    carry = step(pl.multiple_of(qi * t, t), (zero_o, zero_c, zero_o, zero_c), past)

    def body(it, carry):
        return step(pl.multiple_of((qi - it) * t, t), carry, None)

    carry = lax.fori_loop(1, qi + 1, body, carry)

    km = km_ref[...]
    vm = vm_ref[...]
    outs = []
    for h in range(2):
        o, _ = block(qh[h], km, vm, suffix_mat_meta, carry[2 * h], carry[2 * h + 1], None)
        outs.append(o)
    o_ref[...] = jnp.where(head_lanes[0], outs[0], outs[1]).astype(BF16)


def _attention(q, k, v, km, vm, bsz, seq):
    t = ATT_TILE
    n_pairs = SB_WIDTH // LANES
    q3 = q.reshape(bsz, seq, SB_WIDTH)
    k3 = k.reshape(bsz, seq, SB_WIDTH)
    v3 = v.reshape(bsz, seq, SB_WIDTH)
    o = pl.pallas_call(
        _attn_kernel,
        grid=(bsz, n_pairs, seq // t),
        in_specs=[pl.BlockSpec((None, t, LANES), lambda b, p, i: (b, i, p)),
                  pl.BlockSpec((None, seq, LANES), lambda b, p, i: (b, 0, p)),
                  pl.BlockSpec((None, seq, LANES), lambda b, p, i: (b, 0, p)),
                  pl.BlockSpec((N_META, LANES), lambda b, p, i: (0, p)),
                  pl.BlockSpec((N_META, LANES), lambda b, p, i: (0, p))],
        out_specs=pl.BlockSpec((None, t, LANES), lambda b, p, i: (b, i, p)),
        out_shape=jax.ShapeDtypeStruct((bsz, seq, SB_WIDTH), BF16),
        compiler_params=_params("arbitrary", "arbitrary", "arbitrary"),
    )(q3, k3, v3, km, vm)
    return o.reshape(bsz * seq, SB_WIDTH)


MIX_TILE = 256


def _mixer_kernel(conv_ref, o_ref, gate_ref, x_ref, convm_ref, cw_ref, wa_ref, wb_ref, wo_ref,
                  g2_ref, wq_ref, x1_ref, h2_ref, qp_ref, cu_scr):
    ti = pl.program_id(1)
    tl = MIX_TILE
    cw = CONV_WIDTH
    hist = SUBLANES

    @pl.when(ti == 0)
    def _():
        cu_scr[0:hist, :] = (convm_ref[N_META - hist:, cw:2 * cw] * convm_ref[N_META - hist:, 2 * cw:])

    @pl.when(ti != 0)
    def _():
        cu_scr[0:hist, :] = cu_scr[tl:tl + hist, :]

    cu_scr[hist:, :] = conv_ref[:, cw:2 * cw] * conv_ref[:, 2 * cw:]
    conv = (cu_scr[pl.ds(hist - 2, tl), :] * cw_ref[0:1, :]
            + cu_scr[pl.ds(hist - 1, tl), :] * cw_ref[1:2, :]
            + cu_scr[pl.ds(hist, tl), :] * cw_ref[2:3, :])
    y_a = jnp.dot((conv_ref[:, :cw] * conv).astype(BF16), wa_ref[...], preferred_element_type=F32)
    y_b = jnp.dot(o_ref[...], wb_ref[...], preferred_element_type=F32)
    mixed = (jax.nn.sigmoid(gate_ref[:, :D_MODEL]) * y_a
             + jax.nn.sigmoid(gate_ref[:, D_MODEL:]) * y_b)
    x1 = x_ref[...] + jnp.dot(mixed.astype(BF16), wo_ref[...], preferred_element_type=F32)
    x1_ref[...] = x1
    h2 = _rms(x1, g2_ref[...]).astype(BF16)
    h2_ref[...] = h2
    qp_ref[...] = jnp.dot(h2, wq_ref[...], preferred_element_type=F32).astype(BF16)


def _mixer(conv3, o, gates, x2d, conv_meta, conv_w, wa, wb, wo, g2, wq, bsz, seq):
    tl = MIX_TILE
    nt = seq // tl
    n = bsz * seq
    row = lambda width: pl.BlockSpec((tl, width), lambda b, i: (b * nt + i, 0))
    full = lambda a: pl.BlockSpec(a.shape, lambda b, i: (0,) * a.ndim)
    return pl.pallas_call(
        _mixer_kernel,
        grid=(bsz, nt),
        in_specs=[row(3 * CONV_WIDTH), row(SB_WIDTH), row(GATE_WIDTH), row(D_MODEL),
                  full(conv_meta), full(conv_w), full(wa), full(wb), full(wo), full(g2), full(wq)],
        out_specs=[row(D_MODEL), row(D_MODEL), row(PEER_Q_WIDTH)],
        out_shape=[jax.ShapeDtypeStruct((n, D_MODEL), F32),
                   jax.ShapeDtypeStruct((n, D_MODEL), BF16),
                   jax.ShapeDtypeStruct((n, PEER_Q_WIDTH), BF16)],
        scratch_shapes=[pltpu.VMEM((tl + SUBLANES, CONV_WIDTH), F32)],
        compiler_params=_params("arbitrary", "arbitrary"),
    )(conv3, o, gates, x2d, conv_meta, conv_w, wa, wb, wo, g2, wq)


ROUTE_TILE = 128


def _top16(x):
    rows = x.shape[0]
    row_id = lax.broadcasted_iota(jnp.int32, x.shape, 0)
    vals, ids = [], []
    for _ in range(PEER_TOPK):
        m = jnp.max(x, axis=0, keepdims=True)
        idx = jnp.min(jnp.where(x == m, row_id, rows), axis=0, keepdims=True)
        vals.append(m)
        ids.append(idx)
        x = jnp.where(row_id == idx, -jnp.inf, x)
    return jnp.concatenate(vals, axis=0), jnp.concatenate(ids, axis=0)


def _pick(table, sel):
    out = jnp.zeros_like(table)
    for c in range(PEER_TOPK):
        out = jnp.where(sel == c, table[c:c + 1, :], out)
    return out


def _route_kernel(qp_ref, sk_ref, a_ref, b_ref, g_ref):
    i1_all, i2_all, gate_all = [], [], []
    for h in range(PEER_HEADS):
        tops = []
        for p in range(2):
            hp = 2 * h + p
            q_hp = qp_ref[:, hp * PEER_HALF:(hp + 1) * PEER_HALF]
            s_t = lax.dot_general(sk_ref[hp], q_hp, _NT, preferred_element_type=F32)
            tops.append(_top16(s_t))
        (s0, i0), (s1, i1) = tops
        cand = jnp.concatenate([s0[c:c + 1, :] + s1 for c in range(PEER_TOPK)], axis=0)
        score, ci = _top16(cand)
        i1_all.append(_pick(i0, ci // PEER_TOPK))
        i2_all.append(_pick(i1, ci % PEER_TOPK))
        e = jnp.exp(score - jnp.max(score, axis=0, keepdims=True))
        gate_all.append(e / jnp.sum(e, axis=0, keepdims=True))
    a_ref[...] = jnp.concatenate(i1_all, axis=0).T
    b_ref[...] = jnp.concatenate(i2_all, axis=0).T
    g_ref[...] = jnp.concatenate(gate_all, axis=0).T


def _route(qp, subkeys_bf):
    n = qp.shape[0]
    tt = ROUTE_TILE
    sel = pl.BlockSpec((tt, N_SEL), lambda i: (i, 0))
    return pl.pallas_call(
        _route_kernel,
        grid=(n // tt,),
        in_specs=[pl.BlockSpec((tt, PEER_Q_WIDTH), lambda i: (i, 0)),
                  pl.BlockSpec(subkeys_bf.shape, lambda i: (0, 0, 0))],
        out_specs=[sel, sel, sel],
        out_shape=[jax.ShapeDtypeStruct((n, N_SEL), jnp.int32),
                   jax.ShapeDtypeStruct((n, N_SEL), jnp.int32),
                   jax.ShapeDtypeStruct((n, N_SEL), F32)],
        compiler_params=_params("arbitrary"),
    )(qp, subkeys_bf)


SCATTER_TILE = 128
SCATTER_STRIDE = SCATTER_TILE + 1


def _scatter_kernel(a_ref, b_ref, g_ref, w_ref, scr):
    nk = PEER_N_KEYS
    key_id = lax.broadcasted_iota(jnp.int32, (nk, N_SEL), 0)

    def body(t, _):
        a_row = a_ref[pl.ds(t, 1), :]
        b_row = b_ref[pl.ds(t, 1), :]
        g_row = g_ref[pl.ds(t, 1), :]
        x = jnp.where(a_row == key_id, g_row, 0.0).astype(BF16)
        y = jnp.where(b_row == key_id, 1.0, 0.0).astype(BF16)
        w_t = lax.dot_general(x, y, _NT, preferred_element_type=F32)
        scr[pl.ds(t, nk, stride=SCATTER_STRIDE), :] = w_t
        return 0

    lax.fori_loop(0, SCATTER_TILE, body, 0)
    for j in range(nk):
        w_ref[:, j * nk:(j + 1) * nk] = scr[pl.ds(j * SCATTER_STRIDE, SCATTER_TILE), :].astype(BF16)


def _scatter(a, b, g):
    n = a.shape[0]
    tt = SCATTER_TILE
    sel = pl.BlockSpec((tt, N_SEL), lambda i: (i, 0))
    return pl.pallas_call(
        _scatter_kernel,
        grid=(n // tt,),
        in_specs=[sel, sel, sel],
        out_specs=pl.BlockSpec((tt, PEER_N_EXPERTS), lambda i: (i, 0)),
        out_shape=jax.ShapeDtypeStruct((n, PEER_N_EXPERTS), BF16),
        scratch_shapes=[pltpu.VMEM((PEER_N_KEYS * SCATTER_STRIDE, PEER_N_KEYS), F32)],
        compiler_params=_params("arbitrary"),
    )(a, b, g)


EXP_TOKENS = 1024
EXP_BLOCK = 512
EXP_ROWS = 256


def _experts_kernel(h_ref, u_ref, v_ref, w_ref, x1_ref, gf_ref, out_ref, acc_ref):
    j = pl.program_id(1)

    @pl.when(j == 0)
    def _():
        acc_ref[...] = jnp.zeros_like(acc_ref)

    for r in range(EXP_TOKENS // EXP_ROWS):
        rows = pl.ds(r * EXP_ROWS, EXP_ROWS)
        act = jax.nn.gelu(lax.dot_general(h_ref[rows, :], u_ref[...], _NT, preferred_element_type=F32))
        z = (w_ref[rows, :].astype(F32) * act).astype(BF16)
        acc_ref[rows, :] += jnp.dot(z, v_ref[...], preferred_element_type=F32)

    @pl.when(j == pl.num_programs(1) - 1)
    def _():
        out_ref[...] = _rms(x1_ref[...] + acc_ref[...], gf_ref[...])


def _experts(h2, u_bf, v_bf, w, x1, gf):
    n = h2.shape[0]
    tt, eb = EXP_TOKENS, EXP_BLOCK
    tok = lambda width: pl.BlockSpec((tt, width), lambda i, j: (i, 0))
    return pl.pallas_call(
        _experts_kernel,
        grid=(n // tt, PEER_N_EXPERTS // eb),
        in_specs=[tok(D_MODEL),
                  pl.BlockSpec((eb, D_MODEL), lambda i, j: (j, 0)),
                  pl.BlockSpec((eb, D_MODEL), lambda i, j: (j, 0)),
                  pl.BlockSpec((tt, eb), lambda i, j: (i, j)),
                  tok(D_MODEL),
                  pl.BlockSpec((1, D_MODEL), lambda i, j: (0, 0))],
        out_specs=tok(D_MODEL),
        out_shape=jax.ShapeDtypeStruct((n, D_MODEL), F32),
        scratch_shapes=[pltpu.VMEM((tt, D_MODEL), F32)],
        compiler_params=_params("arbitrary", "arbitrary"),
    )(h2, u_bf, v_bf, w, x1, gf)


def kernel(x, meta, norm1_g, w_in, conv_w, w_branch_a, w_branch_b, w_out, norm2_g, peer_w_q,
           peer_subkeys, peer_u, peer_v, final_g):
    bsz, seq, d = x.shape
    n = bsz * seq
    x2d = x.reshape(n, d)
    g1 = norm1_g.reshape(1, d)
    w_in_bf = w_in.astype(BF16)

    conv3, q, k, v, gates = _in_proj(x2d, g1, w_in_bf, 256)
    conv_meta, _, k_meta, v_meta, _ = _in_proj(meta.astype(x.dtype), g1, w_in_bf, N_META)

    o = _attention(q, k, v, k_meta, v_meta, bsz, seq)

    x1, h2, qp = _mixer(conv3, o, gates, x2d, conv_meta, conv_w,
                        w_branch_a.astype(BF16), w_branch_b.astype(BF16), w_out.astype(BF16),
                        norm2_g.reshape(1, d), peer_w_q.astype(BF16), bsz, seq)

    subkeys_bf = peer_subkeys.reshape(PEER_HEADS * 2, PEER_N_KEYS, PEER_HALF).astype(BF16)
    a, b, g = _route(qp, subkeys_bf)
    w = _scatter(a, b, g)
    out = _experts(h2, peer_u.astype(BF16), peer_v.astype(BF16), w, x1, final_g.reshape(1, d))
    return out.reshape(bsz, seq, d)
```

```python
import functools

import jax
import jax.numpy as jnp
from jax import lax
from jax.experimental import pallas as pl
from jax.experimental.pallas import tpu as pltpu

D_MODEL = 1024
N_META = 16
CONV_WIDTH = 512
CONV_K = 3
SB_HEADS = 8
SB_HEAD_DIM = 64
SB_WIDTH = SB_HEADS * SB_HEAD_DIM
PEER_HEADS = 8
PEER_N_KEYS = 128
PEER_N_EXPERTS = PEER_N_KEYS * PEER_N_KEYS
PEER_TOPK = 16
PEER_HALF = 128
PEER_Q_WIDTH = PEER_HEADS * 2 * PEER_HALF
N_SEL = PEER_HEADS * PEER_TOPK
GATE_WIDTH = 2 * D_MODEL
EPS = 1e-6

LANES = 128
SUBLANES = 8
VMEM_LIMIT = 56 * 1024 * 1024

F32 = jnp.float32
BF16 = jnp.bfloat16

_NT = (((1,), (1,)), ((), ()))


def _params(*sem):
    return pltpu.CompilerParams(dimension_semantics=sem, vmem_limit_bytes=VMEM_LIMIT)


def _rms(x, g):
    return x * lax.rsqrt(jnp.mean(x * x, axis=-1, keepdims=True) + EPS) * g


def _in_proj_kernel(x_ref, g_ref, w_ref, conv_ref, q_ref, k_ref, v_ref, gate_ref):
    h = _rms(x_ref[...], g_ref[...]).astype(BF16)
    c0 = 3 * CONV_WIDTH
    conv_ref[...] = jnp.dot(h, w_ref[:, :c0], preferred_element_type=F32)
    scale = SB_HEAD_DIM ** -0.5
    q_ref[...] = (jnp.dot(h, w_ref[:, c0:c0 + SB_WIDTH], preferred_element_type=F32) * scale).astype(BF16)
    k_ref[...] = jnp.dot(h, w_ref[:, c0 + SB_WIDTH:c0 + 2 * SB_WIDTH], preferred_element_type=F32).astype(BF16)
    v_ref[...] = jnp.dot(h, w_ref[:, c0 + 2 * SB_WIDTH:c0 + 3 * SB_WIDTH], preferred_element_type=F32).astype(BF16)
    gate_ref[...] = jnp.dot(h, w_ref[:, c0 + 3 * SB_WIDTH:], preferred_element_type=F32)


def _in_proj(x2d, g, w_in_bf, tm):
    n = x2d.shape[0]
    in_width = w_in_bf.shape[1]
    row = lambda width: pl.BlockSpec((tm, width), lambda i: (i, 0))
    return pl.pallas_call(
        _in_proj_kernel,
        grid=(n // tm,),
        in_specs=[row(D_MODEL),
                  pl.BlockSpec((1, D_MODEL), lambda i: (0, 0)),
                  pl.BlockSpec((D_MODEL, in_width), lambda i: (0, 0))],
        out_specs=[row(3 * CONV_WIDTH), row(SB_WIDTH), row(SB_WIDTH), row(SB_WIDTH), row(GATE_WIDTH)],
        out_shape=[jax.ShapeDtypeStruct((n, 3 * CONV_WIDTH), F32),
                   jax.ShapeDtypeStruct((n, SB_WIDTH), BF16),
                   jax.ShapeDtypeStruct((n, SB_WIDTH), BF16),
                   jax.ShapeDtypeStruct((n, SB_WIDTH), BF16),
                   jax.ShapeDtypeStruct((n, GATE_WIDTH), F32)],
        compiler_params=_params("arbitrary"),
    )(x2d, g, w_in_bf)


ATT_TILE = 256
ATT_HEADS = 4


def _softplus(z):
    return jnp.maximum(z, 0.0) + jnp.log(1.0 + jnp.exp(-jnp.abs(z)))


def _split_bf16(x):
    hi = x.astype(BF16)
    return hi, (x - hi.astype(F32)).astype(BF16)


def _attn_kernel(q_ref, k_ref, v_ref, km_ref, vm_ref, o_ref):
    qi = pl.program_id(2)
    t = ATT_TILE
    heads = range(ATT_HEADS)
    lane = lax.broadcasted_iota(jnp.int32, (1, LANES), 1)
    low_half = lane < SB_HEAD_DIM

    def pair_block(ref_or_val, h):
        p = h // 2
        return ref_or_val[:, p * LANES:(p + 1) * LANES]

    qh = []
    for h in heads:
        q2 = pair_block(q_ref, h)
        qh.append(jnp.where(low_half if h % 2 == 0 else ~low_half, q2, jnp.zeros_like(q2)))

    r = lax.broadcasted_iota(jnp.int32, (t, t), 0)
    c = lax.broadcasted_iota(jnp.int32, (t, t), 1)
    incl = jnp.where(r >= c, 1.0, 0.0).astype(BF16)
    past = c < r
    rm = lax.broadcasted_iota(jnp.int32, (N_META, N_META), 0)
    cm = lax.broadcasted_iota(jnp.int32, (N_META, N_META), 1)
    incl_t = jnp.where(cm >= rm, 1.0, 0.0).astype(BF16)

    def step(start, carry, mask):
        k_t = k_ref[pl.ds(start, t), :]
        v_t = v_ref[pl.ds(start, t), :]
        z = [lax.dot_general(qh[h], pair_block(k_t, h), _NT, preferred_element_type=F32)
             for h in heads]
        sp = [_softplus(z[h]) for h in heads]
        if mask is not None:
            sp = [jnp.where(mask, sp[h], 0.0) for h in heads]
        parts = [_split_bf16(sp[h]) for h in heads]
        suf = [jnp.dot(parts[h][0], incl, preferred_element_type=F32)
               + jnp.dot(parts[h][1], incl, preferred_element_type=F32) for h in heads]
        a = [jnp.exp(z[h] - suf[h] - carry[2 * h + 1]) for h in heads]
        if mask is not None:
            a = [jnp.where(mask, a[h], 0.0) for h in heads]
        out = []
        for h in heads:
            out.append(carry[2 * h] + jnp.dot(a[h].astype(BF16), pair_block(v_t, h), preferred_element_type=F32))
            out.append(carry[2 * h + 1] + jnp.sum(sp[h], axis=1, keepdims=True))
        return tuple(out)

    zero = (jnp.zeros((t, LANES), F32), jnp.zeros((t, 1), F32))
    carry = step(pl.multiple_of(qi * t, t), zero * ATT_HEADS, past)

    def body(it, carry):
        return step(pl.multiple_of((qi - it) * t, t), carry, None)

    carry = lax.fori_loop(1, qi + 1, body, carry)

    outs = []
    for h in heads:
        z_t = lax.dot_general(pair_block(km_ref, h), qh[h], _NT, preferred_element_type=F32)
        hi, lo = _split_bf16(_softplus(z_t))
        suf_t = (jnp.dot(incl_t, hi, preferred_element_type=F32)
                 + jnp.dot(incl_t, lo, preferred_element_type=F32))
        a_t = jnp.exp(z_t - suf_t).astype(BF16)
        o_meta = lax.dot_general(a_t, pair_block(vm_ref, h), (((0,), (0,)), ((), ())),
                                 preferred_element_type=F32)
        outs.append(carry[2 * h] + jnp.exp(-carry[2 * h + 1]) * o_meta)
    for p in range(ATT_HEADS // 2):
        o_ref[:, p * LANES:(p + 1) * LANES] = jnp.where(low_half, outs[2 * p], outs[2 * p + 1]).astype(BF16)


def _attention(q, k, v, km, vm, bsz, seq):
    t = ATT_TILE
    width = ATT_HEADS * SB_HEAD_DIM
    n_groups = SB_WIDTH // width
    q3 = q.reshape(bsz, seq, SB_WIDTH)
    k3 = k.reshape(bsz, seq, SB_WIDTH)
    v3 = v.reshape(bsz, seq, SB_WIDTH)
    o = pl.pallas_call(
        _attn_kernel,
        grid=(bsz, n_groups, seq // t),
        in_specs=[pl.BlockSpec((None, t, width), lambda b, p, i: (b, i, p)),
                  pl.BlockSpec((None, seq, width), lambda b, p, i: (b, 0, p)),
                  pl.BlockSpec((None, seq, width), lambda b, p, i: (b, 0, p)),
                  pl.BlockSpec((N_META, width), lambda b, p, i: (0, p)),
                  pl.BlockSpec((N_META, width), lambda b, p, i: (0, p))],
        out_specs=pl.BlockSpec((None, t, width), lambda b, p, i: (b, i, p)),
        out_shape=jax.ShapeDtypeStruct((bsz, seq, SB_WIDTH), BF16),
        compiler_params=_params("arbitrary", "arbitrary", "arbitrary"),
    )(q3, k3, v3, km, vm)
    return o.reshape(bsz * seq, SB_WIDTH)


MIX_TILE = 256


def _mixer_kernel(conv_ref, o_ref, gate_ref, x_ref, convm_ref, cw_ref, wa_ref, wb_ref, wo_ref,
                  g2_ref, wq_ref, x1_ref, h2_ref, qp_ref, cu_scr):
    ti = pl.program_id(1)
    tl = MIX_TILE
    cw = CONV_WIDTH
    hist = SUBLANES

    @pl.when(ti == 0)
    def _():
        cu_scr[0:hist, :] = (convm_ref[N_META - hist:, cw:2 * cw] * convm_ref[N_META - hist:, 2 * cw:])

    @pl.when(ti != 0)
    def _():
        cu_scr[0:hist, :] = cu_scr[tl:tl + hist, :]

    cu_scr[hist:, :] = conv_ref[:, cw:2 * cw] * conv_ref[:, 2 * cw:]
    conv = (cu_scr[pl.ds(hist - 2, tl), :] * cw_ref[0:1, :]
            + cu_scr[pl.ds(hist - 1, tl), :] * cw_ref[1:2, :]
            + cu_scr[pl.ds(hist, tl), :] * cw_ref[2:3, :])
    y_a = jnp.dot((conv_ref[:, :cw] * conv).astype(BF16), wa_ref[...], preferred_element_type=F32)
    y_b = jnp.dot(o_ref[...], wb_ref[...], preferred_element_type=F32)
    mixed = (jax.nn.sigmoid(gate_ref[:, :D_MODEL]) * y_a
             + jax.nn.sigmoid(gate_ref[:, D_MODEL:]) * y_b)
    x1 = x_ref[...] + jnp.dot(mixed.astype(BF16), wo_ref[...], preferred_element_type=F32)
    x1_ref[...] = x1
    h2 = _rms(x1, g2_ref[...]).astype(BF16)
    h2_ref[...] = h2
    qp_ref[...] = jnp.dot(h2, wq_ref[...], preferred_element_type=F32).astype(BF16)


def _mixer(conv3, o, gates, x2d, conv_meta, conv_w, wa, wb, wo, g2, wq, bsz, seq):
    tl = MIX_TILE
    nt = seq // tl
    n = bsz * seq
    row = lambda width: pl.BlockSpec((tl, width), lambda b, i: (b * nt + i, 0))
    full = lambda a: pl.BlockSpec(a.shape, lambda b, i: (0,) * a.ndim)
    return pl.pallas_call(
        _mixer_kernel,
        grid=(bsz, nt),
        in_specs=[row(3 * CONV_WIDTH), row(SB_WIDTH), row(GATE_WIDTH), row(D_MODEL),
                  full(conv_meta), full(conv_w), full(wa), full(wb), full(wo), full(g2), full(wq)],
        out_specs=[row(D_MODEL), row(D_MODEL), row(PEER_Q_WIDTH)],
        out_shape=[jax.ShapeDtypeStruct((n, D_MODEL), F32),
                   jax.ShapeDtypeStruct((n, D_MODEL), BF16),
                   jax.ShapeDtypeStruct((n, PEER_Q_WIDTH), BF16)],
        scratch_shapes=[pltpu.VMEM((tl + SUBLANES, CONV_WIDTH), F32)],
        compiler_params=_params("arbitrary", "arbitrary"),
    )(conv3, o, gates, x2d, conv_meta, conv_w, wa, wb, wo, g2, wq)


ROUTE_TILE = 128


def _top16(x):
    rows = x.shape[0]
    row_id = lax.broadcasted_iota(jnp.int32, x.shape, 0)
    vals, ids = [], []
    for _ in range(PEER_TOPK):
        m = jnp.max(x, axis=0, keepdims=True)
        idx = jnp.min(jnp.where(x == m, row_id, rows), axis=0, keepdims=True)
        vals.append(m)
        ids.append(idx)
        x = jnp.where(row_id == idx, -jnp.inf, x)
    return jnp.concatenate(vals, axis=0), jnp.concatenate(ids, axis=0)


def _pick(table, sel):
    out = jnp.zeros_like(table)
    for c in range(PEER_TOPK):
        out = jnp.where(sel == c, table[c:c + 1, :], out)
    return out


def _route_kernel(qp_ref, sk_ref, a_ref, b_ref, g_ref):
    i1_all, i2_all, gate_all = [], [], []
    for h in range(PEER_HEADS):
        tops = []
        for p in range(2):
            hp = 2 * h + p
            q_hp = qp_ref[:, hp * PEER_HALF:(hp + 1) * PEER_HALF]
            s_t = lax.dot_general(sk_ref[hp], q_hp, _NT, preferred_element_type=F32)
            tops.append(_top16(s_t))
        (s0, i0), (s1, i1) = tops
        cand = jnp.concatenate([s0[c:c + 1, :] + s1 for c in range(PEER_TOPK)], axis=0)
        score, ci = _top16(cand)
        i1_all.append(_pick(i0, ci // PEER_TOPK))
        i2_all.append(_pick(i1, ci % PEER_TOPK))
        e = jnp.exp(score - jnp.max(score, axis=0, keepdims=True))
        gate_all.append(e / jnp.sum(e, axis=0, keepdims=True))
    a_ref[...] = jnp.concatenate(i1_all, axis=0).T
    b_ref[...] = jnp.concatenate(i2_all, axis=0).T
    g_ref[...] = jnp.concatenate(gate_all, axis=0).T


def _route(qp, subkeys_bf):
    n = qp.shape[0]
    tt = ROUTE_TILE
    sel = pl.BlockSpec((tt, N_SEL), lambda i: (i, 0))
    return pl.pallas_call(
        _route_kernel,
        grid=(n // tt,),
        in_specs=[pl.BlockSpec((tt, PEER_Q_WIDTH), lambda i: (i, 0)),
                  pl.BlockSpec(subkeys_bf.shape, lambda i: (0, 0, 0))],
        out_specs=[sel, sel, sel],
        out_shape=[jax.ShapeDtypeStruct((n, N_SEL), jnp.int32),
                   jax.ShapeDtypeStruct((n, N_SEL), jnp.int32),
                   jax.ShapeDtypeStruct((n, N_SEL), F32)],
        compiler_params=_params("arbitrary"),
    )(qp, subkeys_bf)


SCATTER_TILE = 128
SCATTER_STRIDE = SCATTER_TILE + 1


def _scatter_kernel(a_ref, b_ref, g_ref, w_ref, scr):
    nk = PEER_N_KEYS
    key_id = lax.broadcasted_iota(jnp.int32, (nk, N_SEL), 0)

    def body(t, _):
        a_row = a_ref[pl.ds(t, 1), :]
        b_row = b_ref[pl.ds(t, 1), :]
        g_row = g_ref[pl.ds(t, 1), :]
        x = jnp.where(a_row == key_id, g_row, 0.0).astype(BF16)
        y = jnp.where(b_row == key_id, 1.0, 0.0).astype(BF16)
        w_t = lax.dot_general(x, y, _NT, preferred_element_type=F32)
        scr[pl.ds(t, nk, stride=SCATTER_STRIDE), :] = w_t
        return 0

    lax.fori_loop(0, SCATTER_TILE, body, 0, unroll=8)
    for j in range(nk):
        w_ref[:, j * nk:(j + 1) * nk] = scr[pl.ds(j * SCATTER_STRIDE, SCATTER_TILE), :].astype(BF16)


def _scatter(a, b, g):
    n = a.shape[0]
    tt = SCATTER_TILE
    sel = pl.BlockSpec((tt, N_SEL), lambda i: (i, 0))
    return pl.pallas_call(
        _scatter_kernel,
        grid=(n // tt,),
        in_specs=[sel, sel, sel],
        out_specs=pl.BlockSpec((tt, PEER_N_EXPERTS), lambda i: (i, 0)),
        out_shape=jax.ShapeDtypeStruct((n, PEER_N_EXPERTS), BF16),
        scratch_shapes=[pltpu.VMEM((PEER_N_KEYS * SCATTER_STRIDE, PEER_N_KEYS), F32)],
        compiler_params=_params("arbitrary"),
    )(a, b, g)


EXP_TOKENS = 1024
EXP_BLOCK = 512
EXP_ROWS = 256


def _experts_kernel(h_ref, u_ref, v_ref, w_ref, x1_ref, gf_ref, out_ref, acc_ref):
    j = pl.program_id(1)

    @pl.when(j == 0)
    def _():
        acc_ref[...] = jnp.zeros_like(acc_ref)

    for r in range(EXP_TOKENS // EXP_ROWS):
        rows = pl.ds(r * EXP_ROWS, EXP_ROWS)
        act = jax.nn.gelu(lax.dot_general(h_ref[rows, :], u_ref[...], _NT, preferred_element_type=F32))
        z = (w_ref[rows, :].astype(F32) * act).astype(BF16)
        acc_ref[rows, :] += jnp.dot(z, v_ref[...], preferred_element_type=F32)

    @pl.when(j == pl.num_programs(1) - 1)
    def _():
        out_ref[...] = _rms(x1_ref[...] + acc_ref[...], gf_ref[...])


def _experts(h2, u_bf, v_bf, w, x1, gf):
    n = h2.shape[0]
    tt, eb = EXP_TOKENS, EXP_BLOCK
    tok = lambda width: pl.BlockSpec((tt, width), lambda i, j: (i, 0))
    return pl.pallas_call(
        _experts_kernel,
        grid=(n // tt, PEER_N_EXPERTS // eb),
        in_specs=[tok(D_MODEL),
                  pl.BlockSpec((eb, D_MODEL), lambda i, j: (j, 0)),
                  pl.BlockSpec((eb, D_MODEL), lambda i, j: (j, 0)),
                  pl.BlockSpec((tt, eb), lambda i, j: (i, j)),
                  tok(D_MODEL),
                  pl.BlockSpec((1, D_MODEL), lambda i, j: (0, 0))],
        out_specs=tok(D_MODEL),
        out_shape=jax.ShapeDtypeStruct((n, D_MODEL), F32),
        scratch_shapes=[pltpu.VMEM((tt, D_MODEL), F32)],
        compiler_params=_params("arbitrary", "arbitrary"),
    )(h2, u_bf, v_bf, w, x1, gf)


def kernel(x, meta, norm1_g, w_in, conv_w, w_branch_a, w_branch_b, w_out, norm2_g, peer_w_q,
           peer_subkeys, peer_u, peer_v, final_g):
    bsz, seq, d = x.shape
    n = bsz * seq
    x2d = x.reshape(n, d)
    g1 = norm1_g.reshape(1, d)
    w_in_bf = w_in.astype(BF16)

    conv3, q, k, v, gates = _in_proj(x2d, g1, w_in_bf, 256)
    conv_meta, _, k_meta, v_meta, _ = _in_proj(meta.astype(x.dtype), g1, w_in_bf, N_META)

    o = _attention(q, k, v, k_meta, v_meta, bsz, seq)

    x1, h2, qp = _mixer(conv3, o, gates, x2d, conv_meta, conv_w,
                        w_branch_a.astype(BF16), w_branch_b.astype(BF16), w_out.astype(BF16),
                        norm2_g.reshape(1, d), peer_w_q.astype(BF16), bsz, seq)

    subkeys_bf = peer_subkeys.reshape(PEER_HEADS * 2, PEER_N_KEYS, PEER_HALF).astype(BF16)
    a, b, g = _route(qp, subkeys_bf)
    w = _scatter(a, b, g)
    out = _experts(h2, peer_u.astype(BF16), peer_v.astype(BF16), w, x1, final_g.reshape(1, d))
    return out.reshape(bsz, seq, d)
```

```python
import functools

import jax
import jax.numpy as jnp
from jax import lax
from jax.experimental import pallas as pl
from jax.experimental.pallas import tpu as pltpu

D_MODEL = 1024
N_META = 16
CONV_WIDTH = 512
CONV_K = 3
SB_HEADS = 8
SB_HEAD_DIM = 64
SB_WIDTH = SB_HEADS * SB_HEAD_DIM
PEER_HEADS = 8
PEER_N_KEYS = 128
PEER_N_EXPERTS = PEER_N_KEYS * PEER_N_KEYS
PEER_TOPK = 16
PEER_HALF = 128
PEER_Q_WIDTH = PEER_HEADS * 2 * PEER_HALF
N_SEL = PEER_HEADS * PEER_TOPK
GATE_WIDTH = 2 * D_MODEL
EPS = 1e-6

LANES = 128
SUBLANES = 8
VMEM_LIMIT = 56 * 1024 * 1024

F32 = jnp.float32
BF16 = jnp.bfloat16

_NT = (((1,), (1,)), ((), ()))


def _params(*sem):
    return pltpu.CompilerParams(dimension_semantics=sem, vmem_limit_bytes=VMEM_LIMIT)


def _rms(x, g):
    return x * lax.rsqrt(jnp.mean(x * x, axis=-1, keepdims=True) + EPS) * g


def _in_proj_kernel(x_ref, g_ref, w_ref, conv_ref, q_ref, k_ref, v_ref, gate_ref):
    h = _rms(x_ref[...], g_ref[...]).astype(BF16)
    c0 = 3 * CONV_WIDTH
    conv_ref[...] = jnp.dot(h, w_ref[:, :c0], preferred_element_type=F32)
    scale = SB_HEAD_DIM ** -0.5
    q_ref[...] = (jnp.dot(h, w_ref[:, c0:c0 + SB_WIDTH], preferred_element_type=F32) * scale).astype(BF16)
    k_ref[...] = jnp.dot(h, w_ref[:, c0 + SB_WIDTH:c0 + 2 * SB_WIDTH], preferred_element_type=F32).astype(BF16)
    v_ref[...] = jnp.dot(h, w_ref[:, c0 + 2 * SB_WIDTH:c0 + 3 * SB_WIDTH], preferred_element_type=F32).astype(BF16)
    gate_ref[...] = jnp.dot(h, w_ref[:, c0 + 3 * SB_WIDTH:], preferred_element_type=F32)


def _in_proj(x2d, g, w_in_bf, tm):
    n = x2d.shape[0]
    in_width = w_in_bf.shape[1]
    row = lambda width: pl.BlockSpec((tm, width), lambda i: (i, 0))
    return pl.pallas_call(
        _in_proj_kernel,
        grid=(n // tm,),
        in_specs=[row(D_MODEL),
                  pl.BlockSpec((1, D_MODEL), lambda i: (0, 0)),
                  pl.BlockSpec((D_MODEL, in_width), lambda i: (0, 0))],
        out_specs=[row(3 * CONV_WIDTH), row(SB_WIDTH), row(SB_WIDTH), row(SB_WIDTH), row(GATE_WIDTH)],
        out_shape=[jax.ShapeDtypeStruct((n, 3 * CONV_WIDTH), F32),
                   jax.ShapeDtypeStruct((n, SB_WIDTH), BF16),
                   jax.ShapeDtypeStruct((n, SB_WIDTH), BF16),
                   jax.ShapeDtypeStruct((n, SB_WIDTH), BF16),
                   jax.ShapeDtypeStruct((n, GATE_WIDTH), F32)],
        compiler_params=_params("arbitrary"),
    )(x2d, g, w_in_bf)


ATT_TILE = 256
ATT_HEADS = 4


def _softplus(z):
    return jnp.maximum(z, 0.0) + jnp.log(1.0 + jnp.exp(-jnp.abs(z)))


def _split_bf16(x):
    hi = x.astype(BF16)
    return hi, (x - hi.astype(F32)).astype(BF16)


def _attn_kernel(q_ref, k_ref, v_ref, km_ref, vm_ref, o_ref):
    qi = pl.program_id(2)
    t = ATT_TILE
    heads = range(ATT_HEADS)
    lane = lax.broadcasted_iota(jnp.int32, (1, LANES), 1)
    low_half = lane < SB_HEAD_DIM

    def pair_block(ref_or_val, h):
        p = h // 2
        return ref_or_val[:, p * LANES:(p + 1) * LANES]

    qh = []
    for h in heads:
        q2 = pair_block(q_ref, h)
        qh.append(jnp.where(low_half if h % 2 == 0 else ~low_half, q2, jnp.zeros_like(q2)))

    r = lax.broadcasted_iota(jnp.int32, (t, t), 0)
    c = lax.broadcasted_iota(jnp.int32, (t, t), 1)
    incl = jnp.where(r >= c, 1.0, 0.0).astype(BF16)
    past = c < r
    rm = lax.broadcasted_iota(jnp.int32, (N_META, N_META), 0)
    cm = lax.broadcasted_iota(jnp.int32, (N_META, N_META), 1)
    incl_t = jnp.where(cm >= rm, 1.0, 0.0).astype(BF16)

    def step(start, carry, mask):
        k_t = k_ref[pl.ds(start, t), :]
        v_t = v_ref[pl.ds(start, t), :]
        z = [lax.dot_general(qh[h], pair_block(k_t, h), _NT, preferred_element_type=F32)
             for h in heads]
        sp = [_softplus(z[h]) for h in heads]
        if mask is not None:
            sp = [jnp.where(mask, sp[h], 0.0) for h in heads]
        parts = [_split_bf16(sp[h]) for h in heads]
        suf = [jnp.dot(parts[h][0], incl, preferred_element_type=F32)
               + jnp.dot(parts[h][1], incl, preferred_element_type=F32) for h in heads]
        a = [jnp.exp(z[h] - suf[h] - carry[2 * h + 1]) for h in heads]
        if mask is not None:
            a = [jnp.where(mask, a[h], 0.0) for h in heads]
        out = []
        for h in heads:
            out.append(carry[2 * h] + jnp.dot(a[h].astype(BF16), pair_block(v_t, h), preferred_element_type=F32))
            out.append(carry[2 * h + 1] + jnp.sum(sp[h], axis=1, keepdims=True))
        return tuple(out)

    zero = (jnp.zeros((t, LANES), F32), jnp.zeros((t, 1), F32))
    carry = step(pl.multiple_of(qi * t, t), zero * ATT_HEADS, past)

    def body(it, carry):
        return step(pl.multiple_of((qi - it) * t, t), carry, None)

    carry = lax.fori_loop(1, qi + 1, body, carry)

    outs = []
    for h in heads:
        z_t = lax.dot_general(pair_block(km_ref, h), qh[h], _NT, preferred_element_type=F32)
        hi, lo = _split_bf16(_softplus(z_t))
        suf_t = (jnp.dot(incl_t, hi, preferred_element_type=F32)
                 + jnp.dot(incl_t, lo, preferred_element_type=F32))
        a_t = jnp.exp(z_t - suf_t).astype(BF16)
        o_meta = lax.dot_general(a_t, pair_block(vm_ref, h), (((0,), (0,)), ((), ())),
                                 preferred_element_type=F32)
        outs.append(carry[2 * h] + jnp.exp(-carry[2 * h + 1]) * o_meta)
    for p in range(ATT_HEADS // 2):
        o_ref[:, p * LANES:(p + 1) * LANES] = jnp.where(low_half, outs[2 * p], outs[2 * p + 1]).astype(BF16)


def _attention(q, k, v, km, vm, bsz, seq):
    t = ATT_TILE
    width = ATT_HEADS * SB_HEAD_DIM
    n_groups = SB_WIDTH // width
    q3 = q.reshape(bsz, seq, SB_WIDTH)
    k3 = k.reshape(bsz, seq, SB_WIDTH)
    v3 = v.reshape(bsz, seq, SB_WIDTH)
    o = pl.pallas_call(
        _attn_kernel,
        grid=(bsz, n_groups, seq // t),
        in_specs=[pl.BlockSpec((None, t, width), lambda b, p, i: (b, i, p)),
                  pl.BlockSpec((None, seq, width), lambda b, p, i: (b, 0, p)),
                  pl.BlockSpec((None, seq, width), lambda b, p, i: (b, 0, p)),
                  pl.BlockSpec((N_META, width), lambda b, p, i: (0, p)),
                  pl.BlockSpec((N_META, width), lambda b, p, i: (0, p))],
        out_specs=pl.BlockSpec((None, t, width), lambda b, p, i: (b, i, p)),
        out_shape=jax.ShapeDtypeStruct((bsz, seq, SB_WIDTH), BF16),
        compiler_params=_params("arbitrary", "arbitrary", "arbitrary"),
    )(q3, k3, v3, km, vm)
    return o.reshape(bsz * seq, SB_WIDTH)


MIX_TILE = 256


def _mixer_kernel(conv_ref, o_ref, gate_ref, x_ref, convm_ref, cw_ref, wa_ref, wb_ref, wo_ref,
                  g2_ref, wq_ref, x1_ref, h2_ref, qp_ref, cu_scr):
    ti = pl.program_id(1)
    tl = MIX_TILE
    cw = CONV_WIDTH
    hist = SUBLANES

    @pl.when(ti == 0)
    def _():
        cu_scr[0:hist, :] = (convm_ref[N_META - hist:, cw:2 * cw] * convm_ref[N_META - hist:, 2 * cw:])

    @pl.when(ti != 0)
    def _():
        cu_scr[0:hist, :] = cu_scr[tl:tl + hist, :]

    cu_scr[hist:, :] = conv_ref[:, cw:2 * cw] * conv_ref[:, 2 * cw:]
    conv = (cu_scr[pl.ds(hist - 2, tl), :] * cw_ref[0:1, :]
            + cu_scr[pl.ds(hist - 1, tl), :] * cw_ref[1:2, :]
            + cu_scr[pl.ds(hist, tl), :] * cw_ref[2:3, :])
    y_a = jnp.dot((conv_ref[:, :cw] * conv).astype(BF16), wa_ref[...], preferred_element_type=F32)
    y_b = jnp.dot(o_ref[...], wb_ref[...], preferred_element_type=F32)
    mixed = (jax.nn.sigmoid(gate_ref[:, :D_MODEL]) * y_a
             + jax.nn.sigmoid(gate_ref[:, D_MODEL:]) * y_b)
    x1 = x_ref[...] + jnp.dot(mixed.astype(BF16), wo_ref[...], preferred_element_type=F32)
    x1_ref[...] = x1
    h2 = _rms(x1, g2_ref[...]).astype(BF16)
    h2_ref[...] = h2
    qp_ref[...] = jnp.dot(h2, wq_ref[...], preferred_element_type=F32).astype(BF16)


def _mixer(conv3, o, gates, x2d, conv_meta, conv_w, wa, wb, wo, g2, wq, bsz, seq):
    tl = MIX_TILE
    nt = seq // tl
    n = bsz * seq
    row = lambda width: pl.BlockSpec((tl, width), lambda b, i: (b * nt + i, 0))
    full = lambda a: pl.BlockSpec(a.shape, lambda b, i: (0,) * a.ndim)
    return pl.pallas_call(
        _mixer_kernel,
        grid=(bsz, nt),
        in_specs=[row(3 * CONV_WIDTH), row(SB_WIDTH), row(GATE_WIDTH), row(D_MODEL),
                  full(conv_meta), full(conv_w), full(wa), full(wb), full(wo), full(g2), full(wq)],
        out_specs=[row(D_MODEL), row(D_MODEL), row(PEER_Q_WIDTH)],
        out_shape=[jax.ShapeDtypeStruct((n, D_MODEL), F32),
                   jax.ShapeDtypeStruct((n, D_MODEL), BF16),
                   jax.ShapeDtypeStruct((n, PEER_Q_WIDTH), BF16)],
        scratch_shapes=[pltpu.VMEM((tl + SUBLANES, CONV_WIDTH), F32)],
        compiler_params=_params("arbitrary", "arbitrary"),
    )(conv3, o, gates, x2d, conv_meta, conv_w, wa, wb, wo, g2, wq)


ROUTE_TILE = 128


def _top16(x, row_id):
    vals, ids = [], []
    for _ in range(PEER_TOPK):
        m = jnp.max(x, axis=0, keepdims=True)
        idx = jnp.min(jnp.where(x == m, row_id, jnp.inf), axis=0, keepdims=True)
        vals.append(m)
        ids.append(idx)
        x = jnp.where(row_id == idx, -jnp.inf, x)
    return jnp.concatenate(vals, axis=0), jnp.concatenate(ids, axis=0).astype(jnp.int32)


def _pick(table, sel):
    out = jnp.zeros_like(table)
    for c in range(PEER_TOPK):
        out = jnp.where(sel == c, table[c:c + 1, :], out)
    return out


def _pair_candidates(s0, s1):
    tokens = s0.shape[1]
    sub = lax.broadcasted_iota(jnp.int32, (SUBLANES, tokens), 0)
    subf = sub.astype(F32)
    k = PEER_TOPK
    vals = [s0[0:1, :] + s1, s0[1:2, :] + s1[:SUBLANES, :]]
    ids = [lax.broadcasted_iota(jnp.int32, (k, tokens), 0).astype(F32), subf + float(k)]
    for c1 in range(2, SUBLANES):
        n_valid = k // (c1 + 1)
        vals.append(jnp.where(sub < n_valid, s0[c1:c1 + 1, :] + s1[:SUBLANES, :], -jnp.inf))
        ids.append(subf + float(c1 * k))
    vals.append(s0[SUBLANES:, :] + s1[0:1, :])
    ids.append((subf + float(SUBLANES)) * float(k))
    return jnp.concatenate(vals, axis=0), jnp.concatenate(ids, axis=0)


def _route_kernel(qp_ref, sk_ref, a_ref, b_ref, g_ref):
    tokens = qp_ref.shape[0]
    key_id = lax.broadcasted_iota(jnp.int32, (PEER_N_KEYS, tokens), 0).astype(F32)
    i1_all, i2_all, gate_all = [], [], []
    for h in range(PEER_HEADS):
        tops = []
        for p in range(2):
            hp = 2 * h + p
            q_hp = qp_ref[:, hp * PEER_HALF:(hp + 1) * PEER_HALF]
            s_t = lax.dot_general(sk_ref[hp], q_hp, _NT, preferred_element_type=F32)
            tops.append(_top16(s_t, key_id))
        (s0, i0), (s1, i1) = tops
        score, ci = _top16(*_pair_candidates(s0, s1))
        i1_all.append(_pick(i0, ci // PEER_TOPK))
        i2_all.append(_pick(i1, ci % PEER_TOPK))
        e = jnp.exp(score - jnp.max(score, axis=0, keepdims=True))
        gate_all.append(e / jnp.sum(e, axis=0, keepdims=True))
    a_ref[...] = jnp.concatenate(i1_all, axis=0).T
    b_ref[...] = jnp.concatenate(i2_all, axis=0).T
    g_ref[...] = jnp.concatenate(gate_all, axis=0).T


def _route(qp, subkeys_bf):
    n = qp.shape[0]
    tt = ROUTE_TILE
    sel = pl.BlockSpec((tt, N_SEL), lambda i: (i, 0))
    return pl.pallas_call(
        _route_kernel,
        grid=(n // tt,),
        in_specs=[pl.BlockSpec((tt, PEER_Q_WIDTH), lambda i: (i, 0)),
                  pl.BlockSpec(subkeys_bf.shape, lambda i: (0, 0, 0))],
        out_specs=[sel, sel, sel],
        out_shape=[jax.ShapeDtypeStruct((n, N_SEL), jnp.int32),
                   jax.ShapeDtypeStruct((n, N_SEL), jnp.int32),
                   jax.ShapeDtypeStruct((n, N_SEL), F32)],
        compiler_params=_params("arbitrary"),
    )(qp, subkeys_bf)


SCATTER_TILE = 128
SCATTER_STRIDE = SCATTER_TILE + 1


def _scatter_kernel(a_ref, b_ref, g_ref, w_ref, scr):
    nk = PEER_N_KEYS
    key_id = lax.broadcasted_iota(jnp.int32, (nk, N_SEL), 0)

    def body(t, _):
        a_row = a_ref[pl.ds(t, 1), :]
        b_row = b_ref[pl.ds(t, 1), :]
        g_row = g_ref[pl.ds(t, 1), :]
        x = jnp.where(a_row == key_id, g_row, 0.0).astype(BF16)
        y = jnp.where(b_row == key_id, 1.0, 0.0).astype(BF16)
        w_t = lax.dot_general(x, y, _NT, preferred_element_type=F32)
        scr[pl.ds(t, nk, stride=SCATTER_STRIDE), :] = w_t
        return 0

    lax.fori_loop(0, SCATTER_TILE, body, 0, unroll=8)
    for j in range(nk):
        w_ref[:, j * nk:(j + 1) * nk] = scr[pl.ds(j * SCATTER_STRIDE, SCATTER_TILE), :].astype(BF16)


def _scatter(a, b, g):
    n = a.shape[0]
    tt = SCATTER_TILE
    sel = pl.BlockSpec((tt, N_SEL), lambda i: (i, 0))
    return pl.pallas_call(
        _scatter_kernel,
        grid=(n // tt,),
        in_specs=[sel, sel, sel],
        out_specs=pl.BlockSpec((tt, PEER_N_EXPERTS), lambda i: (i, 0)),
        out_shape=jax.ShapeDtypeStruct((n, PEER_N_EXPERTS), BF16),
        scratch_shapes=[pltpu.VMEM((PEER_N_KEYS * SCATTER_STRIDE, PEER_N_KEYS), F32)],
        compiler_params=_params("arbitrary"),
    )(a, b, g)


EXP_TOKENS = 1024
EXP_BLOCK = 1024
EXP_ROWS = 1024


def _experts_kernel(h_ref, u_ref, v_ref, w_ref, x1_ref, gf_ref, out_ref, acc_ref):
    j = pl.program_id(1)

    @pl.when(j == 0)
    def _():
        acc_ref[...] = jnp.zeros_like(acc_ref)

    for r in range(EXP_TOKENS // EXP_ROWS):
        rows = pl.ds(r * EXP_ROWS, EXP_ROWS)
        act = jax.nn.gelu(jnp.dot(h_ref[rows, :], u_ref[...], preferred_element_type=F32))
        z = (w_ref[rows, :].astype(F32) * act).astype(BF16)
        acc_ref[rows, :] += jnp.dot(z, v_ref[...], preferred_element_type=F32)

    @pl.when(j == pl.num_programs(1) - 1)
    def _():
        out_ref[...] = _rms(x1_ref[...] + acc_ref[...], gf_ref[...])


def _experts(h2, u_bf, v_bf, w, x1, gf):
    n = h2.shape[0]
    tt, eb = EXP_TOKENS, EXP_BLOCK
    tok = lambda width: pl.BlockSpec((tt, width), lambda i, j: (i, 0))
    return pl.pallas_call(
        _experts_kernel,
        grid=(n // tt, PEER_N_EXPERTS // eb),
        in_specs=[tok(D_MODEL),
                  pl.BlockSpec((D_MODEL, eb), lambda i, j: (0, j)),
                  pl.BlockSpec((eb, D_MODEL), lambda i, j: (j, 0)),
                  pl.BlockSpec((tt, eb), lambda i, j: (i, j)),
                  tok(D_MODEL),
                  pl.BlockSpec((1, D_MODEL), lambda i, j: (0, 0))],
        out_specs=tok(D_MODEL),
        out_shape=jax.ShapeDtypeStruct((n, D_MODEL), F32),
        scratch_shapes=[pltpu.VMEM((tt, D_MODEL), F32)],
        compiler_params=_params("arbitrary", "arbitrary"),
    )(h2, u_bf, v_bf, w, x1, gf)


def kernel(x, meta, norm1_g, w_in, conv_w, w_branch_a, w_branch_b, w_out, norm2_g, peer_w_q,
           peer_subkeys, peer_u, peer_v, final_g):
    bsz, seq, d = x.shape
    n = bsz * seq
    x2d = x.reshape(n, d)
    g1 = norm1_g.reshape(1, d)
    w_in_bf = w_in.astype(BF16)

    conv3, q, k, v, gates = _in_proj(x2d, g1, w_in_bf, 256)
    conv_meta, _, k_meta, v_meta, _ = _in_proj(meta.astype(x.dtype), g1, w_in_bf, N_META)

    o = _attention(q, k, v, k_meta, v_meta, bsz, seq)

    x1, h2, qp = _mixer(conv3, o, gates, x2d, conv_meta, conv_w,
                        w_branch_a.astype(BF16), w_branch_b.astype(BF16), w_out.astype(BF16),
                        norm2_g.reshape(1, d), peer_w_q.astype(BF16), bsz, seq)

    subkeys_bf = peer_subkeys.reshape(PEER_HEADS * 2, PEER_N_KEYS, PEER_HALF).astype(BF16)
    a, b, g = _route(qp, subkeys_bf)
    w = _scatter(a, b, g)
    out = _experts(h2, peer_u.T.astype(BF16), peer_v.astype(BF16), w, x1, final_g.reshape(1, d))
    return out.reshape(bsz, seq, d)
```

```python
import functools

import jax
import jax.numpy as jnp
from jax import lax
from jax.experimental import pallas as pl
from jax.experimental.pallas import tpu as pltpu

D_MODEL = 1024
N_META = 16
CONV_WIDTH = 512
CONV_K = 3
SB_HEADS = 8
SB_HEAD_DIM = 64
SB_WIDTH = SB_HEADS * SB_HEAD_DIM
PEER_HEADS = 8
PEER_N_KEYS = 128
PEER_N_EXPERTS = PEER_N_KEYS * PEER_N_KEYS
PEER_TOPK = 16
PEER_HALF = 128
PEER_Q_WIDTH = PEER_HEADS * 2 * PEER_HALF
N_SEL = PEER_HEADS * PEER_TOPK
GATE_WIDTH = 2 * D_MODEL
EPS = 1e-6
LOG2E = 1.4426950408889634

LANES = 128
SUBLANES = 8
VMEM_LIMIT = 56 * 1024 * 1024

F32 = jnp.float32
BF16 = jnp.bfloat16

_NT = (((1,), (1,)), ((), ()))


def _params(*sem):
    return pltpu.CompilerParams(dimension_semantics=sem, vmem_limit_bytes=VMEM_LIMIT)


def _rms(x, g):
    return x * lax.rsqrt(jnp.mean(x * x, axis=-1, keepdims=True) + EPS) * g


def _in_proj_kernel(x_ref, g_ref, w_ref, conv_ref, q_ref, k_ref, v_ref, gate_ref):
    h = _rms(x_ref[...], g_ref[...]).astype(BF16)
    c0 = 3 * CONV_WIDTH
    conv_ref[...] = jnp.dot(h, w_ref[:, :c0], preferred_element_type=F32)
    scale = SB_HEAD_DIM ** -0.5
    q_ref[...] = (jnp.dot(h, w_ref[:, c0:c0 + SB_WIDTH], preferred_element_type=F32) * scale).astype(BF16)
    k_ref[...] = jnp.dot(h, w_ref[:, c0 + SB_WIDTH:c0 + 2 * SB_WIDTH], preferred_element_type=F32).astype(BF16)
    v_ref[...] = jnp.dot(h, w_ref[:, c0 + 2 * SB_WIDTH:c0 + 3 * SB_WIDTH], preferred_element_type=F32).astype(BF16)
    gate_ref[...] = jnp.dot(h, w_ref[:, c0 + 3 * SB_WIDTH:], preferred_element_type=F32)


def _in_proj(x2d, g, w_in_bf, tm):
    n = x2d.shape[0]
    in_width = w_in_bf.shape[1]
    row = lambda width: pl.BlockSpec((tm, width), lambda i: (i, 0))
    return pl.pallas_call(
        _in_proj_kernel,
        grid=(n // tm,),
        in_specs=[row(D_MODEL),
                  pl.BlockSpec((1, D_MODEL), lambda i: (0, 0)),
                  pl.BlockSpec((D_MODEL, in_width), lambda i: (0, 0))],
        out_specs=[row(3 * CONV_WIDTH), row(SB_WIDTH), row(SB_WIDTH), row(SB_WIDTH), row(GATE_WIDTH)],
        out_shape=[jax.ShapeDtypeStruct((n, 3 * CONV_WIDTH), F32),
                   jax.ShapeDtypeStruct((n, SB_WIDTH), BF16),
                   jax.ShapeDtypeStruct((n, SB_WIDTH), BF16),
                   jax.ShapeDtypeStruct((n, SB_WIDTH), BF16),
                   jax.ShapeDtypeStruct((n, GATE_WIDTH), F32)],
        compiler_params=_params("arbitrary"),
    )(x2d, g, w_in_bf)


ATT_TILE = 256
ATT_HEADS = 8


def _softplus(z):
    return jnp.maximum(z, 0.0) + jnp.log(1.0 + jnp.exp2(jnp.abs(z) * (-LOG2E)))


def _split_bf16(x):
    hi = x.astype(BF16)
    return hi, (x - hi.astype(F32)).astype(BF16)


def _attn_kernel(q_ref, k_ref, v_ref, km_ref, vm_ref, o_ref, om_scr):
    qi = pl.program_id(2)
    t = ATT_TILE
    heads = range(ATT_HEADS)
    lane = lax.broadcasted_iota(jnp.int32, (1, LANES), 1)
    low_half = lane < SB_HEAD_DIM

    def pair_block(ref_or_val, h):
        p = h // 2
        return ref_or_val[:, p * LANES:(p + 1) * LANES]

    qh = []
    for h in heads:
        q2 = pair_block(q_ref, h)
        qh.append(jnp.where(low_half if h % 2 == 0 else ~low_half, q2, jnp.zeros_like(q2)))

    r = lax.broadcasted_iota(jnp.int32, (t, t), 0)
    c = lax.broadcasted_iota(jnp.int32, (t, t), 1)
    incl = jnp.where(r >= c, 1.0, 0.0).astype(BF16)
    past = c < r
    rm = lax.broadcasted_iota(jnp.int32, (N_META, N_META), 0)
    cm = lax.broadcasted_iota(jnp.int32, (N_META, N_META), 1)
    incl_t = jnp.where(cm >= rm, 1.0, 0.0).astype(BF16)

    def step(start, carry, mask):
        k_t = k_ref[pl.ds(start, t), :]
        v_t = v_ref[pl.ds(start, t), :]
        z = [lax.dot_general(qh[h], pair_block(k_t, h), _NT, preferred_element_type=F32)
             for h in heads]
        sp = [_softplus(z[h]) for h in heads]
        if mask is not None:
            sp = [jnp.where(mask, sp[h], 0.0) for h in heads]
        parts = [_split_bf16(sp[h]) for h in heads]
        suf = [jnp.dot(parts[h][0], incl, preferred_element_type=F32)
               + jnp.dot(parts[h][1], incl, preferred_element_type=F32) for h in heads]
        a = [jnp.exp(z[h] - suf[h] - carry[2 * h + 1]) for h in heads]
        if mask is not None:
            a = [jnp.where(mask, a[h], 0.0) for h in heads]
        out = []
        for h in heads:
            out.append(carry[2 * h] + jnp.dot(a[h].astype(BF16), pair_block(v_t, h), preferred_element_type=F32))
            out.append(carry[2 * h + 1] + suf[h][:, 0:1])
        return tuple(out)

    z_t = [lax.dot_general(pair_block(km_ref, h), qh[h], _NT, preferred_element_type=F32)
           for h in heads]
    parts_t = [_split_bf16(_softplus(z_t[h])) for h in heads]
    suf_t = [jnp.dot(incl_t, parts_t[h][0], preferred_element_type=F32)
             + jnp.dot(incl_t, parts_t[h][1], preferred_element_type=F32) for h in heads]
    a_t = [jnp.exp(z_t[h] - suf_t[h]).astype(BF16) for h in heads]
    for h in heads:
        om_scr[h] = lax.dot_general(a_t[h], pair_block(vm_ref, h), (((0,), (0,)), ((), ())),
                                    preferred_element_type=F32)

    zero = (jnp.zeros((t, LANES), F32), jnp.zeros((t, 1), F32))
    carry = step(pl.multiple_of(qi * t, t), zero * ATT_HEADS, past)

    def body(it, carry):
        return step(pl.multiple_of((qi - it) * t, t), carry, None)

    carry = lax.fori_loop(1, qi + 1, body, carry)

    outs = [carry[2 * h] + jnp.exp(-carry[2 * h + 1]) * om_scr[h] for h in heads]
    for p in range(ATT_HEADS // 2):
        o_ref[:, p * LANES:(p + 1) * LANES] = jnp.where(low_half, outs[2 * p], outs[2 * p + 1]).astype(BF16)


def _attention(q, k, v, km, vm, bsz, seq):
    t = ATT_TILE
    width = ATT_HEADS * SB_HEAD_DIM
    n_groups = SB_WIDTH // width
    q3 = q.reshape(bsz, seq, SB_WIDTH)
    k3 = k.reshape(bsz, seq, SB_WIDTH)
    v3 = v.reshape(bsz, seq, SB_WIDTH)
    o = pl.pallas_call(
        _attn_kernel,
        grid=(bsz, n_groups, seq // t),
        in_specs=[pl.BlockSpec((None, t, width), lambda b, p, i: (b, i, p)),
                  pl.BlockSpec((None, seq, width), lambda b, p, i: (b, 0, p)),
                  pl.BlockSpec((None, seq, width), lambda b, p, i: (b, 0, p)),
                  pl.BlockSpec((N_META, width), lambda b, p, i: (0, p)),
                  pl.BlockSpec((N_META, width), lambda b, p, i: (0, p))],
        out_specs=pl.BlockSpec((None, t, width), lambda b, p, i: (b, i, p)),
        out_shape=jax.ShapeDtypeStruct((bsz, seq, SB_WIDTH), BF16),
        scratch_shapes=[pltpu.VMEM((ATT_HEADS, t, LANES), F32)],
        compiler_params=_params("arbitrary", "arbitrary", "arbitrary"),
    )(q3, k3, v3, km, vm)
    return o.reshape(bsz * seq, SB_WIDTH)


MIX_TILE = 256


def _mixer_kernel(conv_ref, o_ref, gate_ref, x_ref, convm_ref, cw_ref, wa_ref, wb_ref, wo_ref,
                  g2_ref, wq_ref, x1_ref, h2_ref, qp_ref, cu_scr):
    ti = pl.program_id(1)
    tl = MIX_TILE
    cw = CONV_WIDTH
    hist = SUBLANES

    @pl.when(ti == 0)
    def _():
        cu_scr[0:hist, :] = (convm_ref[N_META - hist:, cw:2 * cw] * convm_ref[N_META - hist:, 2 * cw:])

    @pl.when(ti != 0)
    def _():
        cu_scr[0:hist, :] = cu_scr[tl:tl + hist, :]

    cu_scr[hist:, :] = conv_ref[:, cw:2 * cw] * conv_ref[:, 2 * cw:]
    conv = (cu_scr[pl.ds(hist - 2, tl), :] * cw_ref[0:1, :]
            + cu_scr[pl.ds(hist - 1, tl), :] * cw_ref[1:2, :]
            + cu_scr[pl.ds(hist, tl), :] * cw_ref[2:3, :])
    y_a = jnp.dot((conv_ref[:, :cw] * conv).astype(BF16), wa_ref[...], preferred_element_type=F32)
    y_b = jnp.dot(o_ref[...], wb_ref[...], preferred_element_type=F32)
    mixed = (jax.nn.sigmoid(gate_ref[:, :D_MODEL]) * y_a
             + jax.nn.sigmoid(gate_ref[:, D_MODEL:]) * y_b)
    x1 = x_ref[...] + jnp.dot(mixed.astype(BF16), wo_ref[...], preferred_element_type=F32)
    x1_ref[...] = x1
    h2 = _rms(x1, g2_ref[...]).astype(BF16)
    h2_ref[...] = h2
    qp_ref[...] = jnp.dot(h2, wq_ref[...], preferred_element_type=F32).astype(BF16)


def _mixer(conv3, o, gates, x2d, conv_meta, conv_w, wa, wb, wo, g2, wq, bsz, seq):
    tl = MIX_TILE
    nt = seq // tl
    n = bsz * seq
    row = lambda width: pl.BlockSpec((tl, width), lambda b, i: (b * nt + i, 0))
    full = lambda a: pl.BlockSpec(a.shape, lambda b, i: (0,) * a.ndim)
    return pl.pallas_call(
        _mixer_kernel,
        grid=(bsz, nt),
        in_specs=[row(3 * CONV_WIDTH), row(SB_WIDTH), row(GATE_WIDTH), row(D_MODEL),
                  full(conv_meta), full(conv_w), full(wa), full(wb), full(wo), full(g2), full(wq)],
        out_specs=[row(D_MODEL), row(D_MODEL), row(PEER_Q_WIDTH)],
        out_shape=[jax.ShapeDtypeStruct((n, D_MODEL), F32),
                   jax.ShapeDtypeStruct((n, D_MODEL), BF16),
                   jax.ShapeDtypeStruct((n, PEER_Q_WIDTH), BF16)],
        scratch_shapes=[pltpu.VMEM((tl + SUBLANES, CONV_WIDTH), F32)],
        compiler_params=_params("arbitrary", "arbitrary"),
    )(conv3, o, gates, x2d, conv_meta, conv_w, wa, wb, wo, g2, wq)


ROUTE_TILE = 128


def _top16(x, row_id):
    vals, ids = [], []
    for _ in range(PEER_TOPK):
        m = jnp.max(x, axis=0, keepdims=True)
        idx = jnp.min(jnp.where(x == m, row_id, jnp.inf), axis=0, keepdims=True)
        vals.append(m)
        ids.append(idx)
        x = jnp.where(row_id == idx, -jnp.inf, x)
    return jnp.concatenate(vals, axis=0), jnp.concatenate(ids, axis=0).astype(jnp.int32)


def _pick(table, sel):
    out = jnp.zeros_like(table)
    for c in range(PEER_TOPK):
        out = jnp.where(sel == c, table[c:c + 1, :], out)
    return out


def _pair_candidates(s0, s1):
    tokens = s0.shape[1]
    sub = lax.broadcasted_iota(jnp.int32, (SUBLANES, tokens), 0)
    subf = sub.astype(F32)
    k = PEER_TOPK
    vals = [s0[0:1, :] + s1, s0[1:2, :] + s1[:SUBLANES, :]]
    ids = [lax.broadcasted_iota(jnp.int32, (k, tokens), 0).astype(F32), subf + float(k)]
    for c1 in range(2, SUBLANES):
        n_valid = k // (c1 + 1)
        vals.append(jnp.where(sub < n_valid, s0[c1:c1 + 1, :] + s1[:SUBLANES, :], -jnp.inf))
        ids.append(subf + float(c1 * k))
    vals.append(s0[SUBLANES:, :] + s1[0:1, :])
    ids.append((subf + float(SUBLANES)) * float(k))
    return jnp.concatenate(vals, axis=0), jnp.concatenate(ids, axis=0)


def _route_kernel(qp_ref, sk_ref, a_ref, b_ref, g_ref):
    tokens = qp_ref.shape[0]
    key_id = lax.broadcasted_iota(jnp.int32, (PEER_N_KEYS, tokens), 0).astype(F32)
    i1_all, i2_all, gate_all = [], [], []
    for h in range(PEER_HEADS):
        tops = []
        for p in range(2):
            hp = 2 * h + p
            q_hp = qp_ref[:, hp * PEER_HALF:(hp + 1) * PEER_HALF]
            s_t = lax.dot_general(sk_ref[hp], q_hp, _NT, preferred_element_type=F32)
            tops.append(_top16(s_t, key_id))
        (s0, i0), (s1, i1) = tops
        score, ci = _top16(*_pair_candidates(s0, s1))
        i1_all.append(_pick(i0, ci // PEER_TOPK))
        i2_all.append(_pick(i1, ci % PEER_TOPK))
        e = jnp.exp(score - jnp.max(score, axis=0, keepdims=True))
        gate_all.append(e / jnp.sum(e, axis=0, keepdims=True))
    a_ref[...] = jnp.concatenate(i1_all, axis=0).T
    b_ref[...] = jnp.concatenate(i2_all, axis=0).T
    g_ref[...] = jnp.concatenate(gate_all, axis=0).T


def _route(qp, subkeys_bf):
    n = qp.shape[0]
    tt = ROUTE_TILE
    sel = pl.BlockSpec((tt, N_SEL), lambda i: (i, 0))
    return pl.pallas_call(
        _route_kernel,
        grid=(n // tt,),
        in_specs=[pl.BlockSpec((tt, PEER_Q_WIDTH), lambda i: (i, 0)),
                  pl.BlockSpec(subkeys_bf.shape, lambda i: (0, 0, 0))],
        out_specs=[sel, sel, sel],
        out_shape=[jax.ShapeDtypeStruct((n, N_SEL), jnp.int32),
                   jax.ShapeDtypeStruct((n, N_SEL), jnp.int32),
                   jax.ShapeDtypeStruct((n, N_SEL), F32)],
        compiler_params=_params("arbitrary"),
    )(qp, subkeys_bf)


SCATTER_TILE = 128
SCATTER_STRIDE = SCATTER_TILE + 1


def _scatter_kernel(a_ref, b_ref, g_ref, w_ref, scr):
    nk = PEER_N_KEYS
    key_id = lax.broadcasted_iota(jnp.int32, (nk, N_SEL), 0)

    def body(t, _):
        a_row = a_ref[pl.ds(t, 1), :]
        b_row = b_ref[pl.ds(t, 1), :]
        g_row = g_ref[pl.ds(t, 1), :]
        x = jnp.where(a_row == key_id, g_row, 0.0).astype(BF16)
        y = jnp.where(b_row == key_id, 1.0, 0.0).astype(BF16)
        w_t = lax.dot_general(x, y, _NT, preferred_element_type=F32)
        scr[pl.ds(t, nk, stride=SCATTER_STRIDE), :] = w_t
        return 0

    lax.fori_loop(0, SCATTER_TILE, body, 0, unroll=16)
    for j in range(nk):
        w_ref[:, j * nk:(j + 1) * nk] = scr[pl.ds(j * SCATTER_STRIDE, SCATTER_TILE), :].astype(BF16)


def _scatter(a, b, g):
    n = a.shape[0]
    tt = SCATTER_TILE
    sel = pl.BlockSpec((tt, N_SEL), lambda i: (i, 0))
    return pl.pallas_call(
        _scatter_kernel,
        grid=(n // tt,),
        in_specs=[sel, sel, sel],
        out_specs=pl.BlockSpec((tt, PEER_N_EXPERTS), lambda i: (i, 0)),
        out_shape=jax.ShapeDtypeStruct((n, PEER_N_EXPERTS), BF16),
        scratch_shapes=[pltpu.VMEM((PEER_N_KEYS * SCATTER_STRIDE, PEER_N_KEYS), F32)],
        compiler_params=_params("arbitrary"),
    )(a, b, g)


EXP_TOKENS = 1024
EXP_BLOCK = 1024
EXP_ROWS = 1024


def _experts_kernel(h_ref, u_ref, v_ref, w_ref, x1_ref, gf_ref, out_ref, acc_ref):
    j = pl.program_id(1)

    @pl.when(j == 0)
    def _():
        acc_ref[...] = jnp.zeros_like(acc_ref)

    for r in range(EXP_TOKENS // EXP_ROWS):
        rows = pl.ds(r * EXP_ROWS, EXP_ROWS)
        act = jax.nn.gelu(jnp.dot(h_ref[rows, :], u_ref[...], preferred_element_type=F32))
        z = (w_ref[rows, :].astype(F32) * act).astype(BF16)
        acc_ref[rows, :] += jnp.dot(z, v_ref[...], preferred_element_type=F32)

    @pl.when(j == pl.num_programs(1) - 1)
    def _():
        out_ref[...] = _rms(x1_ref[...] + acc_ref[...], gf_ref[...])


def _experts(h2, u_bf, v_bf, w, x1, gf):
    n = h2.shape[0]
    tt, eb = EXP_TOKENS, EXP_BLOCK
    tok = lambda width: pl.BlockSpec((tt, width), lambda i, j: (i, 0))
    return pl.pallas_call(
        _experts_kernel,
        grid=(n // tt, PEER_N_EXPERTS // eb),
        in_specs=[tok(D_MODEL),
                  pl.BlockSpec((D_MODEL, eb), lambda i, j: (0, j)),
                  pl.BlockSpec((eb, D_MODEL), lambda i, j: (j, 0)),
                  pl.BlockSpec((tt, eb), lambda i, j: (i, j)),
                  tok(D_MODEL),
                  pl.BlockSpec((1, D_MODEL), lambda i, j: (0, 0))],
        out_specs=tok(D_MODEL),
        out_shape=jax.ShapeDtypeStruct((n, D_MODEL), F32),
        scratch_shapes=[pltpu.VMEM((tt, D_MODEL), F32)],
        compiler_params=_params("arbitrary", "arbitrary"),
    )(h2, u_bf, v_bf, w, x1, gf)


def kernel(x, meta, norm1_g, w_in, conv_w, w_branch_a, w_branch_b, w_out, norm2_g, peer_w_q,
           peer_subkeys, peer_u, peer_v, final_g):
    bsz, seq, d = x.shape
    n = bsz * seq
    x2d = x.reshape(n, d)
    g1 = norm1_g.reshape(1, d)
    w_in_bf = w_in.astype(BF16)

    conv3, q, k, v, gates = _in_proj(x2d, g1, w_in_bf, 256)
    conv_meta, _, k_meta, v_meta, _ = _in_proj(meta.astype(x.dtype), g1, w_in_bf, N_META)

    o = _attention(q, k, v, k_meta, v_meta, bsz, seq)

    x1, h2, qp = _mixer(conv3, o, gates, x2d, conv_meta, conv_w,
                        w_branch_a.astype(BF16), w_branch_b.astype(BF16), w_out.astype(BF16),
                        norm2_g.reshape(1, d), peer_w_q.astype(BF16), bsz, seq)

    subkeys_bf = peer_subkeys.reshape(PEER_HEADS * 2, PEER_N_KEYS, PEER_HALF).astype(BF16)
    a, b, g = _route(qp, subkeys_bf)
    w = _scatter(a, b, g)
    out = _experts(h2, peer_u.T.astype(BF16), peer_v.astype(BF16), w, x1, final_g.reshape(1, d))
    return out.reshape(bsz, seq, d)
```

```python
import functools

import jax
import jax.numpy as jnp
from jax import lax
from jax.experimental import pallas as pl
from jax.experimental.pallas import tpu as pltpu

D_MODEL = 1024
N_META = 16
CONV_WIDTH = 512
CONV_K = 3
SB_HEADS = 8
SB_HEAD_DIM = 64
SB_WIDTH = SB_HEADS * SB_HEAD_DIM
PEER_HEADS = 8
PEER_N_KEYS = 128
PEER_N_EXPERTS = PEER_N_KEYS * PEER_N_KEYS
PEER_TOPK = 16
PEER_HALF = 128
PEER_Q_WIDTH = PEER_HEADS * 2 * PEER_HALF
N_SEL = PEER_HEADS * PEER_TOPK
GATE_WIDTH = 2 * D_MODEL
EPS = 1e-6
LOG2E = 1.4426950408889634

LANES = 128
SUBLANES = 8
VMEM_LIMIT = 56 * 1024 * 1024

F32 = jnp.float32
BF16 = jnp.bfloat16

_NT = (((1,), (1,)), ((), ()))


def _params(*sem):
    return pltpu.CompilerParams(dimension_semantics=sem, vmem_limit_bytes=VMEM_LIMIT)


def _rms(x, g):
    return x * lax.rsqrt(jnp.mean(x * x, axis=-1, keepdims=True) + EPS) * g


IN_PROJ_TILE = 512

def _in_proj_kernel(x_ref, g_ref, w_ref, conv_ref, q_ref, k_ref, v_ref, gate_ref):
    h = _rms(x_ref[...], g_ref[...]).astype(BF16)
    c0 = 3 * CONV_WIDTH
    conv_ref[...] = jnp.dot(h, w_ref[:, :c0], preferred_element_type=F32)
    scale = SB_HEAD_DIM ** -0.5
    q_ref[...] = (jnp.dot(h, w_ref[:, c0:c0 + SB_WIDTH], preferred_element_type=F32) * scale).astype(BF16)
    k_ref[...] = jnp.dot(h, w_ref[:, c0 + SB_WIDTH:c0 + 2 * SB_WIDTH], preferred_element_type=F32).astype(BF16)
    v_ref[...] = jnp.dot(h, w_ref[:, c0 + 2 * SB_WIDTH:c0 + 3 * SB_WIDTH], preferred_element_type=F32).astype(BF16)
    gate_ref[...] = jnp.dot(h, w_ref[:, c0 + 3 * SB_WIDTH:], preferred_element_type=F32)


def _in_proj(x2d, g, w_in_bf, tm):
    n = x2d.shape[0]
    in_width = w_in_bf.shape[1]
    row = lambda width: pl.BlockSpec((tm, width), lambda i: (i, 0))
    return pl.pallas_call(
        _in_proj_kernel,
        grid=(n // tm,),
        in_specs=[row(D_MODEL),
                  pl.BlockSpec((1, D_MODEL), lambda i: (0, 0)),
                  pl.BlockSpec((D_MODEL, in_width), lambda i: (0, 0))],
        out_specs=[row(3 * CONV_WIDTH), row(SB_WIDTH), row(SB_WIDTH), row(SB_WIDTH), row(GATE_WIDTH)],
        out_shape=[jax.ShapeDtypeStruct((n, 3 * CONV_WIDTH), F32),
                   jax.ShapeDtypeStruct((n, SB_WIDTH), BF16),
                   jax.ShapeDtypeStruct((n, SB_WIDTH), BF16),
                   jax.ShapeDtypeStruct((n, SB_WIDTH), BF16),
                   jax.ShapeDtypeStruct((n, GATE_WIDTH), F32)],
        compiler_params=_params("arbitrary"),
    )(x2d, g, w_in_bf)


ATT_TILE = 256
ATT_HEADS = 8


def _softplus(z):
    return jnp.maximum(z, 0.0) + jnp.log(1.0 + jnp.exp2(jnp.abs(z) * (-LOG2E)))


def _split_bf16(x):
    hi = x.astype(BF16)
    return hi, (x - hi.astype(F32)).astype(BF16)


def _attn_kernel(q_ref, k_ref, v_ref, km_ref, vm_ref, o_ref, om_scr):
    qi = pl.program_id(2)
    t = ATT_TILE
    heads = range(ATT_HEADS)
    lane = lax.broadcasted_iota(jnp.int32, (1, LANES), 1)
    low_half = lane < SB_HEAD_DIM

    def pair_block(ref_or_val, h):
        p = h // 2
        return ref_or_val[:, p * LANES:(p + 1) * LANES]

    qh = []
    for h in heads:
        q2 = pair_block(q_ref, h)
        qh.append(jnp.where(low_half if h % 2 == 0 else ~low_half, q2, jnp.zeros_like(q2)))

    r = lax.broadcasted_iota(jnp.int32, (t, t), 0)
    c = lax.broadcasted_iota(jnp.int32, (t, t), 1)
    incl = jnp.where(r >= c, 1.0, 0.0).astype(BF16)
    past = c < r
    rm = lax.broadcasted_iota(jnp.int32, (N_META, N_META), 0)
    cm = lax.broadcasted_iota(jnp.int32, (N_META, N_META), 1)
    incl_t = jnp.where(cm >= rm, 1.0, 0.0).astype(BF16)

    def step(start, carry, mask):
        k_t = k_ref[pl.ds(start, t), :]
        v_t = v_ref[pl.ds(start, t), :]
        z = [lax.dot_general(qh[h], pair_block(k_t, h), _NT, preferred_element_type=F32)
             for h in heads]
        sp = [_softplus(z[h]) for h in heads]
        if mask is not None:
            sp = [jnp.where(mask, sp[h], 0.0) for h in heads]
        parts = [_split_bf16(sp[h]) for h in heads]
        suf = [jnp.dot(parts[h][0], incl, preferred_element_type=F32)
               + jnp.dot(parts[h][1], incl, preferred_element_type=F32) for h in heads]
        a = [jnp.exp(z[h] - suf[h] - carry[2 * h + 1]) for h in heads]
        if mask is not None:
            a = [jnp.where(mask, a[h], 0.0) for h in heads]
        out = []
        for h in heads:
            out.append(carry[2 * h] + jnp.dot(a[h].astype(BF16), pair_block(v_t, h), preferred_element_type=F32))
            out.append(carry[2 * h + 1] + suf[h][:, 0:1])
        return tuple(out)

    z_t = [lax.dot_general(pair_block(km_ref, h), qh[h], _NT, preferred_element_type=F32)
           for h in heads]
    parts_t = [_split_bf16(_softplus(z_t[h])) for h in heads]
    suf_t = [jnp.dot(incl_t, parts_t[h][0], preferred_element_type=F32)
             + jnp.dot(incl_t, parts_t[h][1], preferred_element_type=F32) for h in heads]
    a_t = [jnp.exp(z_t[h] - suf_t[h]).astype(BF16) for h in heads]
    for h in heads:
        om_scr[h] = lax.dot_general(a_t[h], pair_block(vm_ref, h), (((0,), (0,)), ((), ())),
                                    preferred_element_type=F32)

    zero = (jnp.zeros((t, LANES), F32), jnp.zeros((t, 1), F32))
    carry = step(pl.multiple_of(qi * t, t), zero * ATT_HEADS, past)

    def body(it, carry):
        return step(pl.multiple_of((qi - it) * t, t), carry, None)

    carry = lax.fori_loop(1, qi + 1, body, carry)

    outs = [carry[2 * h] + jnp.exp(-carry[2 * h + 1]) * om_scr[h] for h in heads]
    for p in range(ATT_HEADS // 2):
        o_ref[:, p * LANES:(p + 1) * LANES] = jnp.where(low_half, outs[2 * p], outs[2 * p + 1]).astype(BF16)


def _attention(q, k, v, km, vm, bsz, seq):
    t = ATT_TILE
    width = ATT_HEADS * SB_HEAD_DIM
    n_groups = SB_WIDTH // width
    q3 = q.reshape(bsz, seq, SB_WIDTH)
    k3 = k.reshape(bsz, seq, SB_WIDTH)
    v3 = v.reshape(bsz, seq, SB_WIDTH)
    o = pl.pallas_call(
        _attn_kernel,
        grid=(bsz, n_groups, seq // t),
        in_specs=[pl.BlockSpec((None, t, width), lambda b, p, i: (b, i, p)),
                  pl.BlockSpec((None, seq, width), lambda b, p, i: (b, 0, p)),
                  pl.BlockSpec((None, seq, width), lambda b, p, i: (b, 0, p)),
                  pl.BlockSpec((N_META, width), lambda b, p, i: (0, p)),
                  pl.BlockSpec((N_META, width), lambda b, p, i: (0, p))],
        out_specs=pl.BlockSpec((None, t, width), lambda b, p, i: (b, i, p)),
        out_shape=jax.ShapeDtypeStruct((bsz, seq, SB_WIDTH), BF16),
        scratch_shapes=[pltpu.VMEM((ATT_HEADS, t, LANES), F32)],
        compiler_params=_params("arbitrary", "arbitrary", "arbitrary"),
    )(q3, k3, v3, km, vm)
    return o.reshape(bsz * seq, SB_WIDTH)


MIX_TILE = 512


def _mixer_kernel(conv_ref, o_ref, gate_ref, x_ref, convm_ref, cw_ref, wa_ref, wb_ref, wo_ref,
                  g2_ref, wq_ref, x1_ref, h2_ref, qp_ref, cu_scr):
    ti = pl.program_id(1)
    tl = MIX_TILE
    cw = CONV_WIDTH
    hist = SUBLANES

    @pl.when(ti == 0)
    def _():
        cu_scr[0:hist, :] = (convm_ref[N_META - hist:, cw:2 * cw] * convm_ref[N_META - hist:, 2 * cw:])

    @pl.when(ti != 0)
    def _():
        cu_scr[0:hist, :] = cu_scr[tl:tl + hist, :]

    cu_scr[hist:, :] = conv_ref[:, cw:2 * cw] * conv_ref[:, 2 * cw:]
    conv = (cu_scr[pl.ds(hist - 2, tl), :] * cw_ref[0:1, :]
            + cu_scr[pl.ds(hist - 1, tl), :] * cw_ref[1:2, :]
            + cu_scr[pl.ds(hist, tl), :] * cw_ref[2:3, :])
    y_a = jnp.dot((conv_ref[:, :cw] * conv).astype(BF16), wa_ref[...], preferred_element_type=F32)
    y_b = jnp.dot(o_ref[...], wb_ref[...], preferred_element_type=F32)
    mixed = (jax.nn.sigmoid(gate_ref[:, :D_MODEL]) * y_a
             + jax.nn.sigmoid(gate_ref[:, D_MODEL:]) * y_b)
    x1 = x_ref[...] + jnp.dot(mixed.astype(BF16), wo_ref[...], preferred_element_type=F32)
    x1_ref[...] = x1
    h2 = _rms(x1, g2_ref[...]).astype(BF16)
    h2_ref[...] = h2
    qp_ref[...] = jnp.dot(h2, wq_ref[...], preferred_element_type=F32).astype(BF16)


def _mixer(conv3, o, gates, x2d, conv_meta, conv_w, wa, wb, wo, g2, wq, bsz, seq):
    tl = MIX_TILE
    nt = seq // tl
    n = bsz * seq
    row = lambda width: pl.BlockSpec((tl, width), lambda b, i: (b * nt + i, 0))
    full = lambda a: pl.BlockSpec(a.shape, lambda b, i: (0,) * a.ndim)
    return pl.pallas_call(
        _mixer_kernel,
        grid=(bsz, nt),
        in_specs=[row(3 * CONV_WIDTH), row(SB_WIDTH), row(GATE_WIDTH), row(D_MODEL),
                  full(conv_meta), full(conv_w), full(wa), full(wb), full(wo), full(g2), full(wq)],
        out_specs=[row(D_MODEL), row(D_MODEL), row(PEER_Q_WIDTH)],
        out_shape=[jax.ShapeDtypeStruct((n, D_MODEL), F32),
                   jax.ShapeDtypeStruct((n, D_MODEL), BF16),
                   jax.ShapeDtypeStruct((n, PEER_Q_WIDTH), BF16)],
        scratch_shapes=[pltpu.VMEM((tl + SUBLANES, CONV_WIDTH), F32)],
        compiler_params=_params("arbitrary", "arbitrary"),
    )(conv3, o, gates, x2d, conv_meta, conv_w, wa, wb, wo, g2, wq)


ROUTE_TILE = 128


def _top16(x, row_id, between=None):
    vals, ids = [], []
    for k in range(PEER_TOPK):
        if between is not None:
            between(k)
        m = jnp.max(x, axis=0, keepdims=True)
        idx = jnp.min(jnp.where(x == m, row_id, jnp.inf), axis=0, keepdims=True)
        vals.append(m)
        ids.append(idx)
        x = jnp.where(row_id == idx, -jnp.inf, x)
    return jnp.concatenate(vals, axis=0), jnp.concatenate(ids, axis=0).astype(jnp.int32)


def _pick(table, sel):
    out = jnp.zeros_like(table)
    for c in range(PEER_TOPK):
        out = jnp.where(sel == c, table[c:c + 1, :], out)
    return out


def _pair_candidates(s0, s1):
    tokens = s0.shape[1]
    sub = lax.broadcasted_iota(jnp.int32, (SUBLANES, tokens), 0)
    subf = sub.astype(F32)
    k = PEER_TOPK
    vals = [s0[0:1, :] + s1, s0[1:2, :] + s1[:SUBLANES, :]]
    ids = [lax.broadcasted_iota(jnp.int32, (k, tokens), 0).astype(F32), subf + float(k)]
    for c1 in range(2, SUBLANES):
        n_valid = k // (c1 + 1)
        vals.append(jnp.where(sub < n_valid, s0[c1:c1 + 1, :] + s1[:SUBLANES, :], -jnp.inf))
        ids.append(subf + float(c1 * k))
    vals.append(s0[SUBLANES:, :] + s1[0:1, :])
    ids.append((subf + float(SUBLANES)) * float(k))
    return jnp.concatenate(vals, axis=0), jnp.concatenate(ids, axis=0)


def _route_head(qp_ref, sk_ref, h, key_id, between):
    tops = []
    for p in range(2):
        hp = 2 * h + p
        q_hp = qp_ref[:, hp * PEER_HALF:(hp + 1) * PEER_HALF]
        s_t = lax.dot_general(sk_ref[hp], q_hp, _NT, preferred_element_type=F32)
        tops.append(_top16(s_t, key_id, functools.partial(between, p)))
    (s0, i0), (s1, i1) = tops
    score, ci = _top16(*_pair_candidates(s0, s1))
    e = jnp.exp(score - jnp.max(score, axis=0, keepdims=True))
    return _pick(i0, ci // PEER_TOPK), _pick(i1, ci % PEER_TOPK), e / jnp.sum(e, axis=0, keepdims=True)


SCATTER_STRIDE = ROUTE_TILE + 1


def _scatter_tokens(tokens, a_scr, b_scr, g_scr, scr):
    nk = PEER_N_KEYS
    key_id = lax.broadcasted_iota(jnp.int32, (nk, N_SEL), 0)
    for t in tokens:
        a_row = a_scr[t:t + 1, :]
        b_row = b_scr[t:t + 1, :]
        g_row = g_scr[t:t + 1, :]
        x = jnp.where(a_row == key_id, g_row, 0.0).astype(BF16)
        y = jnp.where(b_row == key_id, 1.0, 0.0).astype(BF16)
        w_t = lax.dot_general(x, y, _NT, preferred_element_type=F32)
        scr[pl.ds(t, nk, stride=SCATTER_STRIDE), :] = w_t


def _route_scatter_kernel(qp_ref, sk_ref, w_ref, a_scr, b_scr, g_scr, scr):
    @pl.when(pl.program_id(0) == 0)
    def _():
        a_scr[...] = jnp.zeros_like(a_scr)
        b_scr[...] = jnp.zeros_like(b_scr)
        g_scr[...] = jnp.zeros_like(g_scr)

    nk = PEER_N_KEYS
    per_head = ROUTE_TILE // PEER_HEADS
    key_id = lax.broadcasted_iota(jnp.int32, (nk, ROUTE_TILE), 0).astype(F32)
    picked = []
    for h in range(PEER_HEADS):
        def scatter_one(p, k, h=h):
            if k % 2 == 0:
                t = h * per_head + p * (per_head // 2) + k // 2
                _scatter_tokens([t], a_scr, b_scr, g_scr, scr)
        picked.append(_route_head(qp_ref, sk_ref, h, key_id, scatter_one))
    for j in range(nk):
        w_ref[:, j * nk:(j + 1) * nk] = scr[pl.ds(j * SCATTER_STRIDE, ROUTE_TILE), :].astype(BF16)
    a_scr[...] = jnp.concatenate([p[0] for p in picked], axis=0).T
    b_scr[...] = jnp.concatenate([p[1] for p in picked], axis=0).T
    g_scr[...] = jnp.concatenate([p[2] for p in picked], axis=0).T


def _route_scatter(qp, subkeys_bf):
    n = qp.shape[0]
    tt = ROUTE_TILE
    n_tiles = n // tt
    return pl.pallas_call(
        _route_scatter_kernel,
        grid=(n_tiles + 1,),
        in_specs=[pl.BlockSpec((tt, PEER_Q_WIDTH), lambda i: (jnp.minimum(i, n_tiles - 1), 0)),
                  pl.BlockSpec(subkeys_bf.shape, lambda i: (0, 0, 0))],
        out_specs=pl.BlockSpec((tt, PEER_N_EXPERTS), lambda i: (jnp.maximum(i - 1, 0), 0)),
        out_shape=jax.ShapeDtypeStruct((n, PEER_N_EXPERTS), BF16),
        scratch_shapes=[pltpu.VMEM((tt, N_SEL), jnp.int32),
                        pltpu.VMEM((tt, N_SEL), jnp.int32),
                        pltpu.VMEM((tt, N_SEL), F32),
                        pltpu.VMEM((PEER_N_KEYS * SCATTER_STRIDE, PEER_N_KEYS), F32)],
        compiler_params=_params("arbitrary"),
    )(qp, subkeys_bf)


EXP_TOKENS = 1024
EXP_BLOCK = 1024
EXP_ROWS = 1024


def _experts_kernel(h_ref, u_ref, v_ref, w_ref, x1_ref, gf_ref, out_ref, acc_ref):
    j = pl.program_id(1)

    @pl.when(j == 0)
    def _():
        acc_ref[...] = jnp.zeros_like(acc_ref)

    for r in range(EXP_TOKENS // EXP_ROWS):
        rows = pl.ds(r * EXP_ROWS, EXP_ROWS)
        act = jax.nn.gelu(jnp.dot(h_ref[rows, :], u_ref[...], preferred_element_type=F32))
        z = (w_ref[rows, :].astype(F32) * act).astype(BF16)
        acc_ref[rows, :] += jnp.dot(z, v_ref[...], preferred_element_type=F32)

    @pl.when(j == pl.num_programs(1) - 1)
    def _():
        out_ref[...] = _rms(x1_ref[...] + acc_ref[...], gf_ref[...])


def _experts(h2, u_bf, v_bf, w, x1, gf):
    n = h2.shape[0]
    tt, eb = EXP_TOKENS, EXP_BLOCK
    tok = lambda width: pl.BlockSpec((tt, width), lambda i, j: (i, 0))
    return pl.pallas_call(
        _experts_kernel,
        grid=(n // tt, PEER_N_EXPERTS // eb),
        in_specs=[tok(D_MODEL),
                  pl.BlockSpec((D_MODEL, eb), lambda i, j: (0, j)),
                  pl.BlockSpec((eb, D_MODEL), lambda i, j: (j, 0)),
                  pl.BlockSpec((tt, eb), lambda i, j: (i, j)),
                  tok(D_MODEL),
                  pl.BlockSpec((1, D_MODEL), lambda i, j: (0, 0))],
        out_specs=tok(D_MODEL),
        out_shape=jax.ShapeDtypeStruct((n, D_MODEL), F32),
        scratch_shapes=[pltpu.VMEM((tt, D_MODEL), F32)],
        compiler_params=_params("arbitrary", "arbitrary"),
    )(h2, u_bf, v_bf, w, x1, gf)


def kernel(x, meta, norm1_g, w_in, conv_w, w_branch_a, w_branch_b, w_out, norm2_g, peer_w_q,
           peer_subkeys, peer_u, peer_v, final_g):
    bsz, seq, d = x.shape
    n = bsz * seq
    x2d = x.reshape(n, d)
    g1 = norm1_g.reshape(1, d)
    w_in_bf = w_in.astype(BF16)

    conv3, q, k, v, gates = _in_proj(x2d, g1, w_in_bf, IN_PROJ_TILE)
    conv_meta, _, k_meta, v_meta, _ = _in_proj(meta.astype(x.dtype), g1, w_in_bf, N_META)

    o = _attention(q, k, v, k_meta, v_meta, bsz, seq)

    x1, h2, qp = _mixer(conv3, o, gates, x2d, conv_meta, conv_w,
                        w_branch_a.astype(BF16), w_branch_b.astype(BF16), w_out.astype(BF16),
                        norm2_g.reshape(1, d), peer_w_q.astype(BF16), bsz, seq)

    subkeys_bf = peer_subkeys.reshape(PEER_HEADS * 2, PEER_N_KEYS, PEER_HALF).astype(BF16)
    w = _route_scatter(qp, subkeys_bf)
    out = _experts(h2, peer_u.T.astype(BF16), peer_v.astype(BF16), w, x1, final_g.reshape(1, d))
    return out.reshape(bsz, seq, d)
```

```python
import functools

import jax
import jax.numpy as jnp
from jax import lax
from jax.experimental import pallas as pl
from jax.experimental.pallas import tpu as pltpu

D_MODEL = 1024
N_META = 16
CONV_WIDTH = 512
CONV_K = 3
SB_HEADS = 8
SB_HEAD_DIM = 64
SB_WIDTH = SB_HEADS * SB_HEAD_DIM
PEER_HEADS = 8
PEER_N_KEYS = 128
PEER_N_EXPERTS = PEER_N_KEYS * PEER_N_KEYS
PEER_TOPK = 16
PEER_HALF = 128
PEER_Q_WIDTH = PEER_HEADS * 2 * PEER_HALF
N_SEL = PEER_HEADS * PEER_TOPK
GATE_WIDTH = 2 * D_MODEL
EPS = 1e-6
LOG2E = 1.4426950408889634

LANES = 128
SUBLANES = 8
VMEM_LIMIT = 56 * 1024 * 1024

F32 = jnp.float32
BF16 = jnp.bfloat16

_NT = (((1,), (1,)), ((), ()))


def _params(*sem):
    return pltpu.CompilerParams(dimension_semantics=sem, vmem_limit_bytes=VMEM_LIMIT)


def _rms(x, g):
    return x * lax.rsqrt(jnp.mean(x * x, axis=-1, keepdims=True) + EPS) * g


IN_PROJ_TILE = 512

def _in_proj_kernel(x_ref, g_ref, w_ref, conv_ref, q_ref, k_ref, v_ref, gate_ref):
    h = _rms(x_ref[...], g_ref[...]).astype(BF16)
    c0 = 3 * CONV_WIDTH
    conv_ref[...] = jnp.dot(h, w_ref[:, :c0], preferred_element_type=F32)
    scale = SB_HEAD_DIM ** -0.5
    q_ref[...] = (jnp.dot(h, w_ref[:, c0:c0 + SB_WIDTH], preferred_element_type=F32) * scale).astype(BF16)
    k_ref[...] = jnp.dot(h, w_ref[:, c0 + SB_WIDTH:c0 + 2 * SB_WIDTH], preferred_element_type=F32).astype(BF16)
    v_ref[...] = jnp.dot(h, w_ref[:, c0 + 2 * SB_WIDTH:c0 + 3 * SB_WIDTH], preferred_element_type=F32).astype(BF16)
    gate_ref[...] = jnp.dot(h, w_ref[:, c0 + 3 * SB_WIDTH:], preferred_element_type=F32)


def _in_proj(x2d, g, w_in_bf, tm):
    n = x2d.shape[0]
    in_width = w_in_bf.shape[1]
    row = lambda width: pl.BlockSpec((tm, width), lambda i: (i, 0))
    return pl.pallas_call(
        _in_proj_kernel,
        grid=(n // tm,),
        in_specs=[row(D_MODEL),
                  pl.BlockSpec((1, D_MODEL), lambda i: (0, 0)),
                  pl.BlockSpec((D_MODEL, in_width), lambda i: (0, 0))],
        out_specs=[row(3 * CONV_WIDTH), row(SB_WIDTH), row(SB_WIDTH), row(SB_WIDTH), row(GATE_WIDTH)],
        out_shape=[jax.ShapeDtypeStruct((n, 3 * CONV_WIDTH), F32),
                   jax.ShapeDtypeStruct((n, SB_WIDTH), BF16),
                   jax.ShapeDtypeStruct((n, SB_WIDTH), BF16),
                   jax.ShapeDtypeStruct((n, SB_WIDTH), BF16),
                   jax.ShapeDtypeStruct((n, GATE_WIDTH), F32)],
        compiler_params=_params("arbitrary"),
    )(x2d, g, w_in_bf)


ATT_TILE = 256
ATT_HEADS = 8


def _softplus(z):
    return jnp.maximum(z, 0.0) + jnp.log(1.0 + jnp.exp2(jnp.abs(z) * (-LOG2E)))


def _split_bf16(x):
    hi = x.astype(BF16)
    return hi, (x - hi.astype(F32)).astype(BF16)


def _attn_kernel(q_ref, k_ref, v_ref, km_ref, vm_ref, o_ref, om_scr):
    qi = pl.program_id(2)
    t = ATT_TILE
    heads = range(ATT_HEADS)
    lane = lax.broadcasted_iota(jnp.int32, (1, LANES), 1)
    low_half = lane < SB_HEAD_DIM

    def pair_block(ref_or_val, h):
        p = h // 2
        return ref_or_val[:, p * LANES:(p + 1) * LANES]

    qh = []
    for h in heads:
        q2 = pair_block(q_ref, h)
        qh.append(jnp.where(low_half if h % 2 == 0 else ~low_half, q2, jnp.zeros_like(q2)))

    r = lax.broadcasted_iota(jnp.int32, (t, t), 0)
    c = lax.broadcasted_iota(jnp.int32, (t, t), 1)
    incl = jnp.where(r >= c, 1.0, 0.0).astype(BF16)
    past = c < r
    rm = lax.broadcasted_iota(jnp.int32, (N_META, N_META), 0)
    cm = lax.broadcasted_iota(jnp.int32, (N_META, N_META), 1)
    incl_t = jnp.where(cm >= rm, 1.0, 0.0).astype(BF16)

    def step(start, carry, mask):
        k_t = k_ref[pl.ds(start, t), :]
        v_t = v_ref[pl.ds(start, t), :]
        z = [lax.dot_general(qh[h], pair_block(k_t, h), _NT, preferred_element_type=F32)
             for h in heads]
        sp = [_softplus(z[h]) for h in heads]
        if mask is not None:
            sp = [jnp.where(mask, sp[h], 0.0) for h in heads]
        parts = [_split_bf16(sp[h]) for h in heads]
        suf = [jnp.dot(parts[h][0], incl, preferred_element_type=F32)
               + jnp.dot(parts[h][1], incl, preferred_element_type=F32) for h in heads]
        a = [jnp.exp(z[h] - suf[h] - carry[2 * h + 1]) for h in heads]
        if mask is not None:
            a = [jnp.where(mask, a[h], 0.0) for h in heads]
        out = []
        for h in heads:
            out.append(carry[2 * h] + jnp.dot(a[h].astype(BF16), pair_block(v_t, h), preferred_element_type=F32))
            out.append(carry[2 * h + 1] + suf[h][:, 0:1])
        return tuple(out)

    z_t = [lax.dot_general(pair_block(km_ref, h), qh[h], _NT, preferred_element_type=F32)
           for h in heads]
    parts_t = [_split_bf16(_softplus(z_t[h])) for h in heads]
    suf_t = [jnp.dot(incl_t, parts_t[h][0], preferred_element_type=F32)
             + jnp.dot(incl_t, parts_t[h][1], preferred_element_type=F32) for h in heads]
    a_t = [jnp.exp(z_t[h] - suf_t[h]).astype(BF16) for h in heads]
    for h in heads:
        om_scr[h] = lax.dot_general(a_t[h], pair_block(vm_ref, h), (((0,), (0,)), ((), ())),
                                    preferred_element_type=F32)

    zero = (jnp.zeros((t, LANES), F32), jnp.zeros((t, 1), F32))
    carry = step(pl.multiple_of(qi * t, t), zero * ATT_HEADS, past)

    def body(it, carry):
        return step(pl.multiple_of((qi - it) * t, t), carry, None)

    carry = lax.fori_loop(1, qi + 1, body, carry)

    outs = [carry[2 * h] + jnp.exp(-carry[2 * h + 1]) * om_scr[h] for h in heads]
    for p in range(ATT_HEADS // 2):
        o_ref[:, p * LANES:(p + 1) * LANES] = jnp.where(low_half, outs[2 * p], outs[2 * p + 1]).astype(BF16)


def _attention(q, k, v, km, vm, bsz, seq):
    t = ATT_TILE
    width = ATT_HEADS * SB_HEAD_DIM
    n_groups = SB_WIDTH // width
    q3 = q.reshape(bsz, seq, SB_WIDTH)
    k3 = k.reshape(bsz, seq, SB_WIDTH)
    v3 = v.reshape(bsz, seq, SB_WIDTH)
    o = pl.pallas_call(
        _attn_kernel,
        grid=(bsz, n_groups, seq // t),
        in_specs=[pl.BlockSpec((None, t, width), lambda b, p, i: (b, i, p)),
                  pl.BlockSpec((None, seq, width), lambda b, p, i: (b, 0, p)),
                  pl.BlockSpec((None, seq, width), lambda b, p, i: (b, 0, p)),
                  pl.BlockSpec((N_META, width), lambda b, p, i: (0, p)),
                  pl.BlockSpec((N_META, width), lambda b, p, i: (0, p))],
        out_specs=pl.BlockSpec((None, t, width), lambda b, p, i: (b, i, p)),
        out_shape=jax.ShapeDtypeStruct((bsz, seq, SB_WIDTH), BF16),
        scratch_shapes=[pltpu.VMEM((ATT_HEADS, t, LANES), F32)],
        compiler_params=_params("arbitrary", "arbitrary", "arbitrary"),
    )(q3, k3, v3, km, vm)
    return o.reshape(bsz * seq, SB_WIDTH)


MIX_TILE = 512


def _mixer_kernel(conv_ref, o_ref, gate_ref, x_ref, convm_ref, cw_ref, wa_ref, wb_ref, wo_ref,
                  g2_ref, wq_ref, x1_ref, h2_ref, qp_ref, cu_scr):
    ti = pl.program_id(1)
    tl = MIX_TILE
    cw = CONV_WIDTH
    hist = SUBLANES

    @pl.when(ti == 0)
    def _():
        cu_scr[0:hist, :] = (convm_ref[N_META - hist:, cw:2 * cw] * convm_ref[N_META - hist:, 2 * cw:])

    @pl.when(ti != 0)
    def _():
        cu_scr[0:hist, :] = cu_scr[tl:tl + hist, :]

    cu_scr[hist:, :] = conv_ref[:, cw:2 * cw] * conv_ref[:, 2 * cw:]
    conv = (cu_scr[pl.ds(hist - 2, tl), :] * cw_ref[0:1, :]
            + cu_scr[pl.ds(hist - 1, tl), :] * cw_ref[1:2, :]
            + cu_scr[pl.ds(hist, tl), :] * cw_ref[2:3, :])
    y_a = jnp.dot((conv_ref[:, :cw] * conv).astype(BF16), wa_ref[...], preferred_element_type=F32)
    y_b = jnp.dot(o_ref[...], wb_ref[...], preferred_element_type=F32)
    mixed = (jax.nn.sigmoid(gate_ref[:, :D_MODEL]) * y_a
             + jax.nn.sigmoid(gate_ref[:, D_MODEL:]) * y_b)
    x1 = x_ref[...] + jnp.dot(mixed.astype(BF16), wo_ref[...], preferred_element_type=F32)
    x1_ref[...] = x1
    h2 = _rms(x1, g2_ref[...]).astype(BF16)
    h2_ref[...] = h2
    qp_ref[...] = jnp.dot(h2, wq_ref[...], preferred_element_type=F32).astype(BF16)


def _mixer(conv3, o, gates, x2d, conv_meta, conv_w, wa, wb, wo, g2, wq, bsz, seq):
    tl = MIX_TILE
    nt = seq // tl
    n = bsz * seq
    row = lambda width: pl.BlockSpec((tl, width), lambda b, i: (b * nt + i, 0))
    full = lambda a: pl.BlockSpec(a.shape, lambda b, i: (0,) * a.ndim)
    return pl.pallas_call(
        _mixer_kernel,
        grid=(bsz, nt),
        in_specs=[row(3 * CONV_WIDTH), row(SB_WIDTH), row(GATE_WIDTH), row(D_MODEL),
                  full(conv_meta), full(conv_w), full(wa), full(wb), full(wo), full(g2), full(wq)],
        out_specs=[row(D_MODEL), row(D_MODEL), row(PEER_Q_WIDTH)],
        out_shape=[jax.ShapeDtypeStruct((n, D_MODEL), F32),
                   jax.ShapeDtypeStruct((n, D_MODEL), BF16),
                   jax.ShapeDtypeStruct((n, PEER_Q_WIDTH), BF16)],
        scratch_shapes=[pltpu.VMEM((tl + SUBLANES, CONV_WIDTH), F32)],
        compiler_params=_params("arbitrary", "arbitrary"),
    )(conv3, o, gates, x2d, conv_meta, conv_w, wa, wb, wo, g2, wq)


ROUTE_TILE = 128


TOPK_GROUP = 4


def _top16_keys(x, between=None):
    g = TOPK_GROUP
    tokens = x.shape[1]
    n_tiles = PEER_N_KEYS // SUBLANES
    sub = lax.broadcasted_iota(jnp.int32, (SUBLANES, tokens), 0).astype(F32)
    val = [x[k * SUBLANES:(k + 1) * SUBLANES, :] for k in range(n_tiles)]
    rid = [sub + float(k * SUBLANES) for k in range(n_tiles)]

    def exchange(lst_v, lst_i, a, b, ordered_ids):
        va, vb, ia, ib = lst_v[a], lst_v[b], lst_i[a], lst_i[b]
        a_first = (va >= vb) if ordered_ids else ((va > vb) | ((va == vb) & (ia < ib)))
        lst_v[a], lst_v[b] = jnp.where(a_first, va, vb), jnp.where(a_first, vb, va)
        lst_i[a], lst_i[b] = jnp.where(a_first, ia, ib), jnp.where(a_first, ib, ia)

    lists_v, lists_i = [], []
    for s in range(0, n_tiles, g):
        lv, li = val[s:s + g], rid[s:s + g]
        for a, b, ordered in ((0, 1, True), (2, 3, True), (0, 2, True), (1, 3, True), (1, 2, False)):
            exchange(lv, li, a, b, ordered)
        lists_v.append(lv)
        lists_i.append(li)

    vals, ids = [], []
    for k in range(PEER_TOPK):
        if between is not None:
            between(k)
        heads_v = [lv[0] for lv in lists_v]
        heads_i = [li[0] for li in lists_i]
        m = jnp.max(functools.reduce(jnp.maximum, heads_v), axis=0, keepdims=True)
        cand = [jnp.where(hv == m, hi, jnp.inf) for hv, hi in zip(heads_v, heads_i)]
        idx = jnp.min(functools.reduce(jnp.minimum, cand), axis=0, keepdims=True)
        vals.append(m)
        ids.append(idx)
        for lv, li in zip(lists_v, lists_i):
            pop = li[0] == idx
            for d in range(g - 1):
                lv[d] = jnp.where(pop, lv[d + 1], lv[d])
                li[d] = jnp.where(pop, li[d + 1], li[d])
            lv[g - 1] = jnp.where(pop, -jnp.inf, lv[g - 1])
    return jnp.concatenate(vals, axis=0), jnp.concatenate(ids, axis=0).astype(jnp.int32)


def _pick(table, sel):
    out = jnp.zeros_like(table)
    for c in range(PEER_TOPK):
        out = jnp.where(sel == c, table[c:c + 1, :], out)
    return out


def _top16_pairs(s0, s1):
    k = PEER_TOPK
    tokens = s0.shape[1]
    sub = lax.broadcasted_iota(jnp.int32, (SUBLANES, tokens), 0)
    subf = sub.astype(F32)
    lo_v = []
    for d in range(k):
        n_c1 = min(SUBLANES, k // (d + 1))
        v = s0[:SUBLANES, :] + s1[d:d + 1, :]
        lo_v.append(v if n_c1 == SUBLANES else jnp.where(sub < n_c1, v, -jnp.inf))
    lo_i = subf * float(k)
    hi_v = s0[SUBLANES:, :] + s1[0:1, :]
    hi_i = (subf + float(SUBLANES)) * float(k)

    vals, ids = [], []
    for j in range(k):
        m = jnp.max(jnp.maximum(lo_v[0], hi_v), axis=0, keepdims=True)
        cand = jnp.minimum(jnp.where(lo_v[0] == m, lo_i, jnp.inf), jnp.where(hi_v == m, hi_i, jnp.inf))
        idx = jnp.min(cand, axis=0, keepdims=True)
        vals.append(m)
        ids.append(idx)
        pop = lo_i == idx
        for d in range(k - 1 - j):
            lo_v[d] = jnp.where(pop, lo_v[d + 1], lo_v[d])
        lo_i = jnp.where(pop, lo_i + 1.0, lo_i)
        hi_v = jnp.where(hi_i == idx, -jnp.inf, hi_v)
    return jnp.concatenate(vals, axis=0), jnp.concatenate(ids, axis=0).astype(jnp.int32)


def _route_head(qp_ref, sk_ref, h, between):
    tops = []
    for p in range(2):
        hp = 2 * h + p
        q_hp = qp_ref[:, hp * PEER_HALF:(hp + 1) * PEER_HALF]
        s_t = lax.dot_general(sk_ref[hp], q_hp, _NT, preferred_element_type=F32)
        tops.append(_top16_keys(s_t, functools.partial(between, p)))
    (s0, i0), (s1, i1) = tops
    score, ci = _top16_pairs(s0, s1)
    e = jnp.exp(score - jnp.max(score, axis=0, keepdims=True))
    return _pick(i0, ci // PEER_TOPK), _pick(i1, ci % PEER_TOPK), e / jnp.sum(e, axis=0, keepdims=True)


SCATTER_STRIDE = ROUTE_TILE + 1


def _scatter_tokens(tokens, a_scr, b_scr, g_scr, scr):
    nk = PEER_N_KEYS
    key_id = lax.broadcasted_iota(jnp.int32, (nk, N_SEL), 0)
    for t in tokens:
        a_row = a_scr[t:t + 1, :]
        b_row = b_scr[t:t + 1, :]
        g_row = g_scr[t:t + 1, :]
        x = jnp.where(a_row == key_id, g_row, 0.0).astype(BF16)
        y = jnp.where(b_row == key_id, 1.0, 0.0).astype(BF16)
        w_t = lax.dot_general(x, y, _NT, preferred_element_type=F32)
        scr[pl.ds(t, nk, stride=SCATTER_STRIDE), :] = w_t


def _route_scatter_kernel(qp_ref, sk_ref, w_ref, a_scr, b_scr, g_scr, scr):
    @pl.when(pl.program_id(0) == 0)
    def _():
        a_scr[...] = jnp.zeros_like(a_scr)
        b_scr[...] = jnp.zeros_like(b_scr)
        g_scr[...] = jnp.zeros_like(g_scr)

    nk = PEER_N_KEYS
    per_head = ROUTE_TILE // PEER_HEADS
    picked = []
    for h in range(PEER_HEADS):
        def scatter_one(p, k, h=h):
            if k % 2 == 0:
                t = h * per_head + p * (per_head // 2) + k // 2
                _scatter_tokens([t], a_scr, b_scr, g_scr, scr)
        picked.append(_route_head(qp_ref, sk_ref, h, scatter_one))
    for j in range(nk):
        w_ref[:, j * nk:(j + 1) * nk] = scr[pl.ds(j * SCATTER_STRIDE, ROUTE_TILE), :].astype(BF16)
    a_scr[...] = jnp.concatenate([p[0] for p in picked], axis=0).T
    b_scr[...] = jnp.concatenate([p[1] for p in picked], axis=0).T
    g_scr[...] = jnp.concatenate([p[2] for p in picked], axis=0).T


def _route_scatter(qp, subkeys_bf):
    n = qp.shape[0]
    tt = ROUTE_TILE
    n_tiles = n // tt
    return pl.pallas_call(
        _route_scatter_kernel,
        grid=(n_tiles + 1,),
        in_specs=[pl.BlockSpec((tt, PEER_Q_WIDTH), lambda i: (jnp.minimum(i, n_tiles - 1), 0)),
                  pl.BlockSpec(subkeys_bf.shape, lambda i: (0, 0, 0))],
        out_specs=pl.BlockSpec((tt, PEER_N_EXPERTS), lambda i: (jnp.maximum(i - 1, 0), 0)),
        out_shape=jax.ShapeDtypeStruct((n, PEER_N_EXPERTS), BF16),
        scratch_shapes=[pltpu.VMEM((tt, N_SEL), jnp.int32),
                        pltpu.VMEM((tt, N_SEL), jnp.int32),
                        pltpu.VMEM((tt, N_SEL), F32),
                        pltpu.VMEM((PEER_N_KEYS * SCATTER_STRIDE, PEER_N_KEYS), F32)],
        compiler_params=_params("arbitrary"),
    )(qp, subkeys_bf)


EXP_TOKENS = 1024
EXP_BLOCK = 1024
EXP_ROWS = 1024


def _experts_kernel(h_ref, u_ref, v_ref, w_ref, x1_ref, gf_ref, out_ref, acc_ref):
    j = pl.program_id(1)

    @pl.when(j == 0)
    def _():
        acc_ref[...] = jnp.zeros_like(acc_ref)

    for r in range(EXP_TOKENS // EXP_ROWS):
        rows = pl.ds(r * EXP_ROWS, EXP_ROWS)
        act = jax.nn.gelu(jnp.dot(h_ref[rows, :], u_ref[...], preferred_element_type=F32))
        z = (w_ref[rows, :].astype(F32) * act).astype(BF16)
        acc_ref[rows, :] += jnp.dot(z, v_ref[...], preferred_element_type=F32)

    @pl.when(j == pl.num_programs(1) - 1)
    def _():
        out_ref[...] = _rms(x1_ref[...] + acc_ref[...], gf_ref[...])


def _experts(h2, u_bf, v_bf, w, x1, gf):
    n = h2.shape[0]
    tt, eb = EXP_TOKENS, EXP_BLOCK
    tok = lambda width: pl.BlockSpec((tt, width), lambda i, j: (i, 0))
    return pl.pallas_call(
        _experts_kernel,
        grid=(n // tt, PEER_N_EXPERTS // eb),
        in_specs=[tok(D_MODEL),
                  pl.BlockSpec((D_MODEL, eb), lambda i, j: (0, j)),
                  pl.BlockSpec((eb, D_MODEL), lambda i, j: (j, 0)),
                  pl.BlockSpec((tt, eb), lambda i, j: (i, j)),
                  tok(D_MODEL),
                  pl.BlockSpec((1, D_MODEL), lambda i, j: (0, 0))],
        out_specs=tok(D_MODEL),
        out_shape=jax.ShapeDtypeStruct((n, D_MODEL), F32),
        scratch_shapes=[pltpu.VMEM((tt, D_MODEL), F32)],
        compiler_params=_params("arbitrary", "arbitrary"),
    )(h2, u_bf, v_bf, w, x1, gf)


def kernel(x, meta, norm1_g, w_in, conv_w, w_branch_a, w_branch_b, w_out, norm2_g, peer_w_q,
           peer_subkeys, peer_u, peer_v, final_g):
    bsz, seq, d = x.shape
    n = bsz * seq
    x2d = x.reshape(n, d)
    g1 = norm1_g.reshape(1, d)
    w_in_bf = w_in.astype(BF16)

    conv3, q, k, v, gates = _in_proj(x2d, g1, w_in_bf, IN_PROJ_TILE)
    conv_meta, _, k_meta, v_meta, _ = _in_proj(meta.astype(x.dtype), g1, w_in_bf, N_META)

    o = _attention(q, k, v, k_meta, v_meta, bsz, seq)

    x1, h2, qp = _mixer(conv3, o, gates, x2d, conv_meta, conv_w,
                        w_branch_a.astype(BF16), w_branch_b.astype(BF16), w_out.astype(BF16),
                        norm2_g.reshape(1, d), peer_w_q.astype(BF16), bsz, seq)

    subkeys_bf = peer_subkeys.reshape(PEER_HEADS * 2, PEER_N_KEYS, PEER_HALF).astype(BF16)
    w = _route_scatter(qp, subkeys_bf)
    out = _experts(h2, peer_u.T.astype(BF16), peer_v.astype(BF16), w, x1, final_g.reshape(1, d))
    return out.reshape(bsz, seq, d)
```

```python
import functools

import jax
import jax.numpy as jnp
from jax import lax
from jax.experimental import pallas as pl
from jax.experimental.pallas import tpu as pltpu

D_MODEL = 1024
N_META = 16
CONV_WIDTH = 512
CONV_K = 3
SB_HEADS = 8
SB_HEAD_DIM = 64
SB_WIDTH = SB_HEADS * SB_HEAD_DIM
PEER_HEADS = 8
PEER_N_KEYS = 128
PEER_N_EXPERTS = PEER_N_KEYS * PEER_N_KEYS
PEER_TOPK = 16
PEER_HALF = 128
PEER_Q_WIDTH = PEER_HEADS * 2 * PEER_HALF
N_SEL = PEER_HEADS * PEER_TOPK
GATE_WIDTH = 2 * D_MODEL
EPS = 1e-6
LOG2E = 1.4426950408889634

LANES = 128
SUBLANES = 8
VMEM_LIMIT = 56 * 1024 * 1024

F32 = jnp.float32
BF16 = jnp.bfloat16

_NT = (((1,), (1,)), ((), ()))


def _params(*sem):
    return pltpu.CompilerParams(dimension_semantics=sem, vmem_limit_bytes=VMEM_LIMIT)


def _rms(x, g):
    return x * lax.rsqrt(jnp.mean(x * x, axis=-1, keepdims=True) + EPS) * g


IN_PROJ_TILE = 512

def _in_proj_kernel(x_ref, g_ref, w_ref, conv_ref, q_ref, k_ref, v_ref, gate_ref):
    h = _rms(x_ref[...], g_ref[...]).astype(BF16)
    c0 = 3 * CONV_WIDTH
    conv_ref[...] = jnp.dot(h, w_ref[:, :c0], preferred_element_type=F32)
    scale = SB_HEAD_DIM ** -0.5
    q_ref[...] = (jnp.dot(h, w_ref[:, c0:c0 + SB_WIDTH], preferred_element_type=F32) * scale).astype(BF16)
    k_ref[...] = jnp.dot(h, w_ref[:, c0 + SB_WIDTH:c0 + 2 * SB_WIDTH], preferred_element_type=F32).astype(BF16)
    v_ref[...] = jnp.dot(h, w_ref[:, c0 + 2 * SB_WIDTH:c0 + 3 * SB_WIDTH], preferred_element_type=F32).astype(BF16)
    gate_ref[...] = jnp.dot(h, w_ref[:, c0 + 3 * SB_WIDTH:], preferred_element_type=F32)


def _in_proj(x2d, g, w_in_bf, tm):
    n = x2d.shape[0]
    in_width = w_in_bf.shape[1]
    row = lambda width: pl.BlockSpec((tm, width), lambda i: (i, 0))
    return pl.pallas_call(
        _in_proj_kernel,
        grid=(n // tm,),
        in_specs=[row(D_MODEL),
                  pl.BlockSpec((1, D_MODEL), lambda i: (0, 0)),
                  pl.BlockSpec((D_MODEL, in_width), lambda i: (0, 0))],
        out_specs=[row(3 * CONV_WIDTH), row(SB_WIDTH), row(SB_WIDTH), row(SB_WIDTH), row(GATE_WIDTH)],
        out_shape=[jax.ShapeDtypeStruct((n, 3 * CONV_WIDTH), F32),
                   jax.ShapeDtypeStruct((n, SB_WIDTH), BF16),
                   jax.ShapeDtypeStruct((n, SB_WIDTH), BF16),
                   jax.ShapeDtypeStruct((n, SB_WIDTH), BF16),
                   jax.ShapeDtypeStruct((n, GATE_WIDTH), F32)],
        compiler_params=_params("arbitrary"),
    )(x2d, g, w_in_bf)


ATT_TILE = 256
ATT_HEADS = 8
UNDERFLOW_CSUM = 150.0


def _softplus(z):
    return jnp.maximum(z, 0.0) + jnp.log(1.0 + jnp.exp2(jnp.abs(z) * (-LOG2E)))


def _split_bf16(x):
    hi = x.astype(BF16)
    return hi, (x - hi.astype(F32)).astype(BF16)


def _attn_kernel(q_ref, k_ref, v_ref, km_ref, vm_ref, o_ref, om_scr):
    qi = pl.program_id(2)
    t = ATT_TILE
    heads = range(ATT_HEADS)
    lane = lax.broadcasted_iota(jnp.int32, (1, LANES), 1)
    low_half = lane < SB_HEAD_DIM

    def pair_block(ref_or_val, h):
        p = h // 2
        return ref_or_val[:, p * LANES:(p + 1) * LANES]

    qh = []
    for h in heads:
        q2 = pair_block(q_ref, h)
        qh.append(jnp.where(low_half if h % 2 == 0 else ~low_half, q2, jnp.zeros_like(q2)))

    r = lax.broadcasted_iota(jnp.int32, (t, t), 0)
    c = lax.broadcasted_iota(jnp.int32, (t, t), 1)
    incl = jnp.where(r >= c, 1.0, 0.0).astype(BF16)
    past = c < r
    rm = lax.broadcasted_iota(jnp.int32, (N_META, N_META), 0)
    cm = lax.broadcasted_iota(jnp.int32, (N_META, N_META), 1)
    incl_t = jnp.where(cm >= rm, 1.0, 0.0).astype(BF16)

    def step(start, carry, mask):
        k_t = k_ref[pl.ds(start, t), :]
        v_t = v_ref[pl.ds(start, t), :]
        z = [lax.dot_general(qh[h], pair_block(k_t, h), _NT, preferred_element_type=F32)
             for h in heads]
        sp = [_softplus(z[h]) for h in heads]
        if mask is not None:
            sp = [jnp.where(mask, sp[h], 0.0) for h in heads]
        parts = [_split_bf16(sp[h]) for h in heads]
        suf = [jnp.dot(parts[h][0], incl, preferred_element_type=F32)
               + jnp.dot(parts[h][1], incl, preferred_element_type=F32) for h in heads]
        a = [jnp.exp(z[h] - suf[h] - carry[2 * h + 1]) for h in heads]
        if mask is not None:
            a = [jnp.where(mask, a[h], 0.0) for h in heads]
        out = []
        for h in heads:
            out.append(carry[2 * h] + jnp.dot(a[h].astype(BF16), pair_block(v_t, h), preferred_element_type=F32))
            out.append(carry[2 * h + 1] + suf[h][:, 0:1])
        return tuple(out)

    z_t = [lax.dot_general(pair_block(km_ref, h), qh[h], _NT, preferred_element_type=F32)
           for h in heads]
    parts_t = [_split_bf16(_softplus(z_t[h])) for h in heads]
    suf_t = [jnp.dot(incl_t, parts_t[h][0], preferred_element_type=F32)
             + jnp.dot(incl_t, parts_t[h][1], preferred_element_type=F32) for h in heads]
    a_t = [jnp.exp(z_t[h] - suf_t[h]).astype(BF16) for h in heads]
    for h in heads:
        om_scr[h] = lax.dot_general(a_t[h], pair_block(vm_ref, h), (((0,), (0,)), ((), ())),
                                    preferred_element_type=F32)

    zero = (jnp.zeros((t, LANES), F32), jnp.zeros((t, 1), F32))
    carry = step(pl.multiple_of(qi * t, t), zero * ATT_HEADS, past)

    def min_csum(carry):
        return jnp.min(functools.reduce(jnp.minimum, [carry[2 * h + 1] for h in heads]))

    def more(state):
        it, smallest, _ = state
        return jnp.logical_and(it <= qi, smallest <= UNDERFLOW_CSUM)

    def body(state):
        it, _, carry = state
        carry = step(pl.multiple_of((qi - it) * t, t), carry, None)
        return it + 1, min_csum(carry), carry

    _, _, carry = lax.while_loop(more, body, (jnp.int32(1), min_csum(carry), carry))

    outs = [carry[2 * h] + jnp.exp(-carry[2 * h + 1]) * om_scr[h] for h in heads]
    for p in range(ATT_HEADS // 2):
        o_ref[:, p * LANES:(p + 1) * LANES] = jnp.where(low_half, outs[2 * p], outs[2 * p + 1]).astype(BF16)


def _attention(q, k, v, km, vm, bsz, seq):
    t = ATT_TILE
    width = ATT_HEADS * SB_HEAD_DIM
    n_groups = SB_WIDTH // width
    q3 = q.reshape(bsz, seq, SB_WIDTH)
    k3 = k.reshape(bsz, seq, SB_WIDTH)
    v3 = v.reshape(bsz, seq, SB_WIDTH)
    o = pl.pallas_call(
        _attn_kernel,
        grid=(bsz, n_groups, seq // t),
        in_specs=[pl.BlockSpec((None, t, width), lambda b, p, i: (b, i, p)),
                  pl.BlockSpec((None, seq, width), lambda b, p, i: (b, 0, p)),
                  pl.BlockSpec((None, seq, width), lambda b, p, i: (b, 0, p)),
                  pl.BlockSpec((N_META, width), lambda b, p, i: (0, p)),
                  pl.BlockSpec((N_META, width), lambda b, p, i: (0, p))],
        out_specs=pl.BlockSpec((None, t, width), lambda b, p, i: (b, i, p)),
        out_shape=jax.ShapeDtypeStruct((bsz, seq, SB_WIDTH), BF16),
        scratch_shapes=[pltpu.VMEM((ATT_HEADS, t, LANES), F32)],
        compiler_params=_params("arbitrary", "arbitrary", "arbitrary"),
    )(q3, k3, v3, km, vm)
    return o.reshape(bsz * seq, SB_WIDTH)


MIX_TILE = 512


def _mixer_kernel(conv_ref, o_ref, gate_ref, x_ref, convm_ref, cw_ref, wa_ref, wb_ref, wo_ref,
                  g2_ref, wq_ref, x1_ref, h2_ref, qp_ref, cu_scr):
    ti = pl.program_id(1)
    tl = MIX_TILE
    cw = CONV_WIDTH
    hist = SUBLANES

    @pl.when(ti == 0)
    def _():
        cu_scr[0:hist, :] = (convm_ref[N_META - hist:, cw:2 * cw] * convm_ref[N_META - hist:, 2 * cw:])

    @pl.when(ti != 0)
    def _():
        cu_scr[0:hist, :] = cu_scr[tl:tl + hist, :]

    cu_scr[hist:, :] = conv_ref[:, cw:2 * cw] * conv_ref[:, 2 * cw:]
    conv = (cu_scr[pl.ds(hist - 2, tl), :] * cw_ref[0:1, :]
            + cu_scr[pl.ds(hist - 1, tl), :] * cw_ref[1:2, :]
            + cu_scr[pl.ds(hist, tl), :] * cw_ref[2:3, :])
    y_a = jnp.dot((conv_ref[:, :cw] * conv).astype(BF16), wa_ref[...], preferred_element_type=F32)
    y_b = jnp.dot(o_ref[...], wb_ref[...], preferred_element_type=F32)
    mixed = (jax.nn.sigmoid(gate_ref[:, :D_MODEL]) * y_a
             + jax.nn.sigmoid(gate_ref[:, D_MODEL:]) * y_b)
    x1 = x_ref[...] + jnp.dot(mixed.astype(BF16), wo_ref[...], preferred_element_type=F32)
    x1_ref[...] = x1
    h2 = _rms(x1, g2_ref[...]).astype(BF16)
    h2_ref[...] = h2
    qp_ref[...] = jnp.dot(h2, wq_ref[...], preferred_element_type=F32).astype(BF16)


def _mixer(conv3, o, gates, x2d, conv_meta, conv_w, wa, wb, wo, g2, wq, bsz, seq):
    tl = MIX_TILE
    nt = seq // tl
    n = bsz * seq
    row = lambda width: pl.BlockSpec((tl, width), lambda b, i: (b * nt + i, 0))
    full = lambda a: pl.BlockSpec(a.shape, lambda b, i: (0,) * a.ndim)
    return pl.pallas_call(
        _mixer_kernel,
        grid=(bsz, nt),
        in_specs=[row(3 * CONV_WIDTH), row(SB_WIDTH), row(GATE_WIDTH), row(D_MODEL),
                  full(conv_meta), full(conv_w), full(wa), full(wb), full(wo), full(g2), full(wq)],
        out_specs=[row(D_MODEL), row(D_MODEL), row(PEER_Q_WIDTH)],
        out_shape=[jax.ShapeDtypeStruct((n, D_MODEL), F32),
                   jax.ShapeDtypeStruct((n, D_MODEL), BF16),
                   jax.ShapeDtypeStruct((n, PEER_Q_WIDTH), BF16)],
        scratch_shapes=[pltpu.VMEM((tl + SUBLANES, CONV_WIDTH), F32)],
        compiler_params=_params("arbitrary", "arbitrary"),
    )(conv3, o, gates, x2d, conv_meta, conv_w, wa, wb, wo, g2, wq)


ROUTE_TILE = 128


TOPK_GROUP = 4


def _top16_keys(x, between=None):
    g = TOPK_GROUP
    tokens = x.shape[1]
    n_tiles = PEER_N_KEYS // SUBLANES
    sub = lax.broadcasted_iota(jnp.int32, (SUBLANES, tokens), 0).astype(F32)
    val = [x[k * SUBLANES:(k + 1) * SUBLANES, :] for k in range(n_tiles)]
    rid = [sub + float(k * SUBLANES) for k in range(n_tiles)]

    def exchange(lst_v, lst_i, a, b, ordered_ids):
        va, vb, ia, ib = lst_v[a], lst_v[b], lst_i[a], lst_i[b]
        a_first = (va >= vb) if ordered_ids else ((va > vb) | ((va == vb) & (ia < ib)))
        lst_v[a], lst_v[b] = jnp.where(a_first, va, vb), jnp.where(a_first, vb, va)
        lst_i[a], lst_i[b] = jnp.where(a_first, ia, ib), jnp.where(a_first, ib, ia)

    lists_v, lists_i = [], []
    for s in range(0, n_tiles, g):
        lv, li = val[s:s + g], rid[s:s + g]
        for a, b, ordered in ((0, 1, True), (2, 3, True), (0, 2, True), (1, 3, True), (1, 2, False)):
            exchange(lv, li, a, b, ordered)
        lists_v.append(lv)
        lists_i.append(li)

    vals, ids = [], []
    for k in range(PEER_TOPK):
        if between is not None:
            between(k)
        heads_v = [lv[0] for lv in lists_v]
        heads_i = [li[0] for li in lists_i]
        m = jnp.max(functools.reduce(jnp.maximum, heads_v), axis=0, keepdims=True)
        cand = [jnp.where(hv == m, hi, jnp.inf) for hv, hi in zip(heads_v, heads_i)]
        idx = jnp.min(functools.reduce(jnp.minimum, cand), axis=0, keepdims=True)
        vals.append(m)
        ids.append(idx)
        for lv, li in zip(lists_v, lists_i):
            pop = li[0] == idx
            for d in range(g - 1):
                lv[d] = jnp.where(pop, lv[d + 1], lv[d])
                li[d] = jnp.where(pop, li[d + 1], li[d])
            lv[g - 1] = jnp.where(pop, -jnp.inf, lv[g - 1])
    return jnp.concatenate(vals, axis=0), jnp.concatenate(ids, axis=0).astype(jnp.int32)


def _pick(table, sel):
    out = jnp.zeros_like(table)
    for c in range(PEER_TOPK):
        out = jnp.where(sel == c, table[c:c + 1, :], out)
    return out


def _top16_pairs(s0, s1):
    k = PEER_TOPK
    tokens = s0.shape[1]
    sub = lax.broadcasted_iota(jnp.int32, (SUBLANES, tokens), 0)
    subf = sub.astype(F32)
    lo_v = []
    for d in range(k):
        n_c1 = min(SUBLANES, k // (d + 1))
        v = s0[:SUBLANES, :] + s1[d:d + 1, :]
        lo_v.append(v if n_c1 == SUBLANES else jnp.where(sub < n_c1, v, -jnp.inf))
    lo_i = subf * float(k)
    hi_v = s0[SUBLANES:, :] + s1[0:1, :]
    hi_i = (subf + float(SUBLANES)) * float(k)

    vals, ids = [], []
    for j in range(k):
        m = jnp.max(jnp.maximum(lo_v[0], hi_v), axis=0, keepdims=True)
        cand = jnp.minimum(jnp.where(lo_v[0] == m, lo_i, jnp.inf), jnp.where(hi_v == m, hi_i, jnp.inf))
        idx = jnp.min(cand, axis=0, keepdims=True)
        vals.append(m)
        ids.append(idx)
        pop = lo_i == idx
        for d in range(k - 1 - j):
            lo_v[d] = jnp.where(pop, lo_v[d + 1], lo_v[d])
        lo_i = jnp.where(pop, lo_i + 1.0, lo_i)
        hi_v = jnp.where(hi_i == idx, -jnp.inf, hi_v)
    return jnp.concatenate(vals, axis=0), jnp.concatenate(ids, axis=0).astype(jnp.int32)


def _route_head(qp_ref, sk_ref, h, between):
    tops = []
    for p in range(2):
        hp = 2 * h + p
        q_hp = qp_ref[:, hp * PEER_HALF:(hp + 1) * PEER_HALF]
        s_t = lax.dot_general(sk_ref[hp], q_hp, _NT, preferred_element_type=F32)
        tops.append(_top16_keys(s_t, functools.partial(between, p)))
    (s0, i0), (s1, i1) = tops
    score, ci = _top16_pairs(s0, s1)
    e = jnp.exp(score - jnp.max(score, axis=0, keepdims=True))
    return _pick(i0, ci // PEER_TOPK), _pick(i1, ci % PEER_TOPK), e / jnp.sum(e, axis=0, keepdims=True)


SCATTER_STRIDE = ROUTE_TILE + 1


def _scatter_tokens(tokens, a_scr, b_scr, g_scr, scr):
    nk = PEER_N_KEYS
    key_id = lax.broadcasted_iota(jnp.int32, (nk, N_SEL), 0)
    for t in tokens:
        a_row = a_scr[t:t + 1, :]
        b_row = b_scr[t:t + 1, :]
        g_row = g_scr[t:t + 1, :]
        x = jnp.where(a_row == key_id, g_row, 0.0).astype(BF16)
        y = jnp.where(b_row == key_id, 1.0, 0.0).astype(BF16)
        w_t = lax.dot_general(x, y, _NT, preferred_element_type=F32)
        scr[pl.ds(t, nk, stride=SCATTER_STRIDE), :] = w_t


def _route_scatter_kernel(qp_ref, sk_ref, w_ref, a_scr, b_scr, g_scr, scr):
    @pl.when(pl.program_id(0) == 0)
    def _():
        a_scr[...] = jnp.zeros_like(a_scr)
        b_scr[...] = jnp.zeros_like(b_scr)
        g_scr[...] = jnp.zeros_like(g_scr)

    nk = PEER_N_KEYS
    per_head = ROUTE_TILE // PEER_HEADS
    picked = []
    for h in range(PEER_HEADS):
        def scatter_one(p, k, h=h):
            if k % 2 == 0:
                t = h * per_head + p * (per_head // 2) + k // 2
                _scatter_tokens([t], a_scr, b_scr, g_scr, scr)
        picked.append(_route_head(qp_ref, sk_ref, h, scatter_one))
    for j in range(nk):
        w_ref[:, j * nk:(j + 1) * nk] = scr[pl.ds(j * SCATTER_STRIDE, ROUTE_TILE), :].astype(BF16)
    a_scr[...] = jnp.concatenate([p[0] for p in picked], axis=0).T
    b_scr[...] = jnp.concatenate([p[1] for p in picked], axis=0).T
    g_scr[...] = jnp.concatenate([p[2] for p in picked], axis=0).T


def _route_scatter(qp, subkeys_bf):
    n = qp.shape[0]
    tt = ROUTE_TILE
    n_tiles = n // tt
    return pl.pallas_call(
        _route_scatter_kernel,
        grid=(n_tiles + 1,),
        in_specs=[pl.BlockSpec((tt, PEER_Q_WIDTH), lambda i: (jnp.minimum(i, n_tiles - 1), 0)),
                  pl.BlockSpec(subkeys_bf.shape, lambda i: (0, 0, 0))],
        out_specs=pl.BlockSpec((tt, PEER_N_EXPERTS), lambda i: (jnp.maximum(i - 1, 0), 0)),
        out_shape=jax.ShapeDtypeStruct((n, PEER_N_EXPERTS), BF16),
        scratch_shapes=[pltpu.VMEM((tt, N_SEL), jnp.int32),
                        pltpu.VMEM((tt, N_SEL), jnp.int32),
                        pltpu.VMEM((tt, N_SEL), F32),
                        pltpu.VMEM((PEER_N_KEYS * SCATTER_STRIDE, PEER_N_KEYS), F32)],
        compiler_params=_params("arbitrary"),
    )(qp, subkeys_bf)


EXP_TOKENS = 1024
EXP_BLOCK = 1024
EXP_ROWS = 1024


def _experts_kernel(h_ref, u_ref, v_ref, w_ref, x1_ref, gf_ref, out_ref, acc_ref):
    j = pl.program_id(1)

    @pl.when(j == 0)
    def _():
        acc_ref[...] = jnp.zeros_like(acc_ref)

    for r in range(EXP_TOKENS // EXP_ROWS):
        rows = pl.ds(r * EXP_ROWS, EXP_ROWS)
        act = jax.nn.gelu(jnp.dot(h_ref[rows, :], u_ref[...], preferred_element_type=F32))
        z = (w_ref[rows, :].astype(F32) * act).astype(BF16)
        acc_ref[rows, :] += jnp.dot(z, v_ref[...], preferred_element_type=F32)

    @pl.when(j == pl.num_programs(1) - 1)
    def _():
        out_ref[...] = _rms(x1_ref[...] + acc_ref[...], gf_ref[...])


def _experts(h2, u_bf, v_bf, w, x1, gf):
    n = h2.shape[0]
    tt, eb = EXP_TOKENS, EXP_BLOCK
    tok = lambda width: pl.BlockSpec((tt, width), lambda i, j: (i, 0))
    return pl.pallas_call(
        _experts_kernel,
        grid=(n // tt, PEER_N_EXPERTS // eb),
        in_specs=[tok(D_MODEL),
                  pl.BlockSpec((D_MODEL, eb), lambda i, j: (0, j)),
                  pl.BlockSpec((eb, D_MODEL), lambda i, j: (j, 0)),
                  pl.BlockSpec((tt, eb), lambda i, j: (i, j)),
                  tok(D_MODEL),
                  pl.BlockSpec((1, D_MODEL), lambda i, j: (0, 0))],
        out_specs=tok(D_MODEL),
        out_shape=jax.ShapeDtypeStruct((n, D_MODEL), F32),
        scratch_shapes=[pltpu.VMEM((tt, D_MODEL), F32)],
        compiler_params=_params("arbitrary", "arbitrary"),
    )(h2, u_bf, v_bf, w, x1, gf)


def kernel(x, meta, norm1_g, w_in, conv_w, w_branch_a, w_branch_b, w_out, norm2_g, peer_w_q,
           peer_subkeys, peer_u, peer_v, final_g):
    bsz, seq, d = x.shape
    n = bsz * seq
    x2d = x.reshape(n, d)
    g1 = norm1_g.reshape(1, d)
    w_in_bf = w_in.astype(BF16)

    conv3, q, k, v, gates = _in_proj(x2d, g1, w_in_bf, IN_PROJ_TILE)
    conv_meta, _, k_meta, v_meta, _ = _in_proj(meta.astype(x.dtype), g1, w_in_bf, N_META)

    o = _attention(q, k, v, k_meta, v_meta, bsz, seq)

    x1, h2, qp = _mixer(conv3, o, gates, x2d, conv_meta, conv_w,
                        w_branch_a.astype(BF16), w_branch_b.astype(BF16), w_out.astype(BF16),
                        norm2_g.reshape(1, d), peer_w_q.astype(BF16), bsz, seq)

    subkeys_bf = peer_subkeys.reshape(PEER_HEADS * 2, PEER_N_KEYS, PEER_HALF).astype(BF16)
    w = _route_scatter(qp, subkeys_bf)
    out = _experts(h2, peer_u.T.astype(BF16), peer_v.astype(BF16), w, x1, final_g.reshape(1, d))
    return out.reshape(bsz, seq, d)
```

```python
import functools

import jax
import jax.numpy as jnp
from jax import lax
from jax.experimental import pallas as pl
from jax.experimental.pallas import tpu as pltpu

D_MODEL = 1024
N_META = 16
CONV_WIDTH = 512
CONV_K = 3
SB_HEADS = 8
SB_HEAD_DIM = 64
SB_WIDTH = SB_HEADS * SB_HEAD_DIM
PEER_HEADS = 8
PEER_N_KEYS = 128
PEER_N_EXPERTS = PEER_N_KEYS * PEER_N_KEYS
PEER_TOPK = 16
PEER_HALF = 128
PEER_Q_WIDTH = PEER_HEADS * 2 * PEER_HALF
N_SEL = PEER_HEADS * PEER_TOPK
GATE_WIDTH = 2 * D_MODEL
EPS = 1e-6
LOG2E = 1.4426950408889634

LANES = 128
SUBLANES = 8
VMEM_LIMIT = 56 * 1024 * 1024

F32 = jnp.float32
BF16 = jnp.bfloat16

_NT = (((1,), (1,)), ((), ()))


def _params(*sem):
    return pltpu.CompilerParams(dimension_semantics=sem, vmem_limit_bytes=VMEM_LIMIT)


def _rms(x, g):
    return x * lax.rsqrt(jnp.mean(x * x, axis=-1, keepdims=True) + EPS) * g


IN_PROJ_TILE = 512

def _in_proj_kernel(x_ref, g_ref, w_ref, conv_ref, q_ref, k_ref, v_ref, gate_ref):
    h = _rms(x_ref[...], g_ref[...]).astype(BF16)
    c0 = 3 * CONV_WIDTH
    conv_ref[...] = jnp.dot(h, w_ref[:, :c0], preferred_element_type=F32)
    scale = SB_HEAD_DIM ** -0.5
    q_ref[...] = (jnp.dot(h, w_ref[:, c0:c0 + SB_WIDTH], preferred_element_type=F32) * scale).astype(BF16)
    k_ref[...] = jnp.dot(h, w_ref[:, c0 + SB_WIDTH:c0 + 2 * SB_WIDTH], preferred_element_type=F32).astype(BF16)
    v_ref[...] = jnp.dot(h, w_ref[:, c0 + 2 * SB_WIDTH:c0 + 3 * SB_WIDTH], preferred_element_type=F32).astype(BF16)
    gate_ref[...] = jnp.dot(h, w_ref[:, c0 + 3 * SB_WIDTH:], preferred_element_type=F32)


def _in_proj(x2d, g, w_in_bf, tm):
    n = x2d.shape[0]
    in_width = w_in_bf.shape[1]
    row = lambda width: pl.BlockSpec((tm, width), lambda i: (i, 0))
    return pl.pallas_call(
        _in_proj_kernel,
        grid=(n // tm,),
        in_specs=[row(D_MODEL),
                  pl.BlockSpec((1, D_MODEL), lambda i: (0, 0)),
                  pl.BlockSpec((D_MODEL, in_width), lambda i: (0, 0))],
        out_specs=[row(3 * CONV_WIDTH), row(SB_WIDTH), row(SB_WIDTH), row(SB_WIDTH), row(GATE_WIDTH)],
        out_shape=[jax.ShapeDtypeStruct((n, 3 * CONV_WIDTH), F32),
                   jax.ShapeDtypeStruct((n, SB_WIDTH), BF16),
                   jax.ShapeDtypeStruct((n, SB_WIDTH), BF16),
                   jax.ShapeDtypeStruct((n, SB_WIDTH), BF16),
                   jax.ShapeDtypeStruct((n, GATE_WIDTH), F32)],
        compiler_params=_params("arbitrary"),
    )(x2d, g, w_in_bf)


ATT_TILE = 256
ATT_HEADS = 8
UNDERFLOW_CSUM = 150.0


def _softplus(z):
    return jnp.maximum(z, 0.0) + jnp.log(1.0 + jnp.exp2(jnp.abs(z) * (-LOG2E)))


def _split_bf16(x):
    hi = x.astype(BF16)
    return hi, (x - hi.astype(F32)).astype(BF16)


def _attn_kernel(q_ref, k_ref, v_ref, km_ref, vm_ref, o_ref, om_scr):
    qi = pl.program_id(2)
    t = ATT_TILE
    heads = range(ATT_HEADS)
    lane = lax.broadcasted_iota(jnp.int32, (1, LANES), 1)
    low_half = lane < SB_HEAD_DIM

    def pair_block(ref_or_val, h):
        p = h // 2
        return ref_or_val[:, p * LANES:(p + 1) * LANES]

    qh = []
    for h in heads:
        q2 = pair_block(q_ref, h)
        qh.append(jnp.where(low_half if h % 2 == 0 else ~low_half, q2, jnp.zeros_like(q2)))

    r = lax.broadcasted_iota(jnp.int32, (t, t), 0)
    c = lax.broadcasted_iota(jnp.int32, (t, t), 1)
    incl = jnp.where(r >= c, 1.0, 0.0).astype(BF16)
    past = c < r
    rm = lax.broadcasted_iota(jnp.int32, (N_META, N_META), 0)
    cm = lax.broadcasted_iota(jnp.int32, (N_META, N_META), 1)
    incl_t = jnp.where(cm >= rm, 1.0, 0.0).astype(BF16)

    def step(start, carry, mask):
        k_t = k_ref[pl.ds(start, t), :]
        v_t = v_ref[pl.ds(start, t), :]
        z = [lax.dot_general(qh[h], pair_block(k_t, h), _NT, preferred_element_type=F32)
             for h in heads]
        sp = [_softplus(z[h]) for h in heads]
        if mask is not None:
            sp = [jnp.where(mask, sp[h], 0.0) for h in heads]
        parts = [_split_bf16(sp[h]) for h in heads]
        suf = [jnp.dot(parts[h][0], incl, preferred_element_type=F32)
               + jnp.dot(parts[h][1], incl, preferred_element_type=F32) for h in heads]
        a = [jnp.exp(z[h] - suf[h] - carry[2 * h + 1]) for h in heads]
        if mask is not None:
            a = [jnp.where(mask, a[h], 0.0) for h in heads]
        out = []
        for h in heads:
            out.append(carry[2 * h] + jnp.dot(a[h].astype(BF16), pair_block(v_t, h), preferred_element_type=F32))
            out.append(carry[2 * h + 1] + suf[h][:, 0:1])
        return tuple(out)

    z_t = [lax.dot_general(pair_block(km_ref, h), qh[h], _NT, preferred_element_type=F32)
           for h in heads]
    parts_t = [_split_bf16(_softplus(z_t[h])) for h in heads]
    suf_t = [jnp.dot(incl_t, parts_t[h][0], preferred_element_type=F32)
             + jnp.dot(incl_t, parts_t[h][1], preferred_element_type=F32) for h in heads]
    a_t = [jnp.exp(z_t[h] - suf_t[h]).astype(BF16) for h in heads]
    for h in heads:
        om_scr[h] = lax.dot_general(a_t[h], pair_block(vm_ref, h), (((0,), (0,)), ((), ())),
                                    preferred_element_type=F32)

    zero = (jnp.zeros((t, LANES), F32), jnp.zeros((t, 1), F32))
    carry = step(pl.multiple_of(qi * t, t), zero * ATT_HEADS, past)

    def min_csum(carry):
        return jnp.min(functools.reduce(jnp.minimum, [carry[2 * h + 1] for h in heads]))

    def more(state):
        it, smallest, _ = state
        return jnp.logical_and(it <= qi, smallest <= UNDERFLOW_CSUM)

    def body(state):
        it, _, carry = state
        carry = step(pl.multiple_of((qi - it) * t, t), carry, None)
        return it + 1, min_csum(carry), carry

    _, _, carry = lax.while_loop(more, body, (jnp.int32(1), min_csum(carry), carry))

    outs = [carry[2 * h] + jnp.exp(-carry[2 * h + 1]) * om_scr[h] for h in heads]
    for p in range(ATT_HEADS // 2):
        o_ref[:, p * LANES:(p + 1) * LANES] = jnp.where(low_half, outs[2 * p], outs[2 * p + 1]).astype(BF16)


def _attention(q, k, v, km, vm, bsz, seq):
    t = ATT_TILE
    width = ATT_HEADS * SB_HEAD_DIM
    n_groups = SB_WIDTH // width
    q3 = q.reshape(bsz, seq, SB_WIDTH)
    k3 = k.reshape(bsz, seq, SB_WIDTH)
    v3 = v.reshape(bsz, seq, SB_WIDTH)
    o = pl.pallas_call(
        _attn_kernel,
        grid=(bsz, n_groups, seq // t),
        in_specs=[pl.BlockSpec((None, t, width), lambda b, p, i: (b, i, p)),
                  pl.BlockSpec((None, seq, width), lambda b, p, i: (b, 0, p)),
                  pl.BlockSpec((None, seq, width), lambda b, p, i: (b, 0, p)),
                  pl.BlockSpec((N_META, width), lambda b, p, i: (0, p)),
                  pl.BlockSpec((N_META, width), lambda b, p, i: (0, p))],
        out_specs=pl.BlockSpec((None, t, width), lambda b, p, i: (b, i, p)),
        out_shape=jax.ShapeDtypeStruct((bsz, seq, SB_WIDTH), BF16),
        scratch_shapes=[pltpu.VMEM((ATT_HEADS, t, LANES), F32)],
        compiler_params=_params("arbitrary", "arbitrary", "arbitrary"),
    )(q3, k3, v3, km, vm)
    return o.reshape(bsz * seq, SB_WIDTH)


MIX_TILE = 512


def _mixer_kernel(conv_ref, o_ref, gate_ref, x_ref, convm_ref, cw_ref, wa_ref, wb_ref, wo_ref,
                  g2_ref, wq_ref, x1_ref, h2_ref, qp_ref, cu_scr):
    ti = pl.program_id(1)
    tl = MIX_TILE
    cw = CONV_WIDTH
    hist = SUBLANES

    @pl.when(ti == 0)
    def _():
        cu_scr[0:hist, :] = (convm_ref[N_META - hist:, cw:2 * cw] * convm_ref[N_META - hist:, 2 * cw:])

    @pl.when(ti != 0)
    def _():
        cu_scr[0:hist, :] = cu_scr[tl:tl + hist, :]

    cu_scr[hist:, :] = conv_ref[:, cw:2 * cw] * conv_ref[:, 2 * cw:]
    conv = (cu_scr[pl.ds(hist - 2, tl), :] * cw_ref[0:1, :]
            + cu_scr[pl.ds(hist - 1, tl), :] * cw_ref[1:2, :]
            + cu_scr[pl.ds(hist, tl), :] * cw_ref[2:3, :])
    y_a = jnp.dot((conv_ref[:, :cw] * conv).astype(BF16), wa_ref[...], preferred_element_type=F32)
    y_b = jnp.dot(o_ref[...], wb_ref[...], preferred_element_type=F32)
    mixed = (jax.nn.sigmoid(gate_ref[:, :D_MODEL]) * y_a
             + jax.nn.sigmoid(gate_ref[:, D_MODEL:]) * y_b)
    x1 = x_ref[...] + jnp.dot(mixed.astype(BF16), wo_ref[...], preferred_element_type=F32)
    x1_ref[...] = x1
    h2 = _rms(x1, g2_ref[...]).astype(BF16)
    h2_ref[...] = h2
    qp_ref[...] = jnp.dot(h2, wq_ref[...], preferred_element_type=F32).astype(BF16)


def _mixer(conv3, o, gates, x2d, conv_meta, conv_w, wa, wb, wo, g2, wq, bsz, seq):
    tl = MIX_TILE
    nt = seq // tl
    n = bsz * seq
    row = lambda width: pl.BlockSpec((tl, width), lambda b, i: (b * nt + i, 0))
    full = lambda a: pl.BlockSpec(a.shape, lambda b, i: (0,) * a.ndim)
    return pl.pallas_call(
        _mixer_kernel,
        grid=(bsz, nt),
        in_specs=[row(3 * CONV_WIDTH), row(SB_WIDTH), row(GATE_WIDTH), row(D_MODEL),
                  full(conv_meta), full(conv_w), full(wa), full(wb), full(wo), full(g2), full(wq)],
        out_specs=[row(D_MODEL), row(D_MODEL), row(PEER_Q_WIDTH)],
        out_shape=[jax.ShapeDtypeStruct((n, D_MODEL), F32),
                   jax.ShapeDtypeStruct((n, D_MODEL), BF16),
                   jax.ShapeDtypeStruct((n, PEER_Q_WIDTH), BF16)],
        scratch_shapes=[pltpu.VMEM((tl + SUBLANES, CONV_WIDTH), F32)],
        compiler_params=_params("arbitrary", "arbitrary"),
    )(conv3, o, gates, x2d, conv_meta, conv_w, wa, wb, wo, g2, wq)


ROUTE_TILE = 128


TOPK_GROUP = 4


def _top16_keys(x, between=None):
    g = TOPK_GROUP
    tokens = x.shape[1]
    n_tiles = PEER_N_KEYS // SUBLANES
    sub = lax.broadcasted_iota(jnp.int32, (SUBLANES, tokens), 0).astype(F32)
    val = [x[k * SUBLANES:(k + 1) * SUBLANES, :] for k in range(n_tiles)]
    rid = [sub + float(k * SUBLANES) for k in range(n_tiles)]

    def exchange(lst_v, lst_i, a, b, ordered_ids):
        va, vb, ia, ib = lst_v[a], lst_v[b], lst_i[a], lst_i[b]
        a_first = (va >= vb) if ordered_ids else ((va > vb) | ((va == vb) & (ia < ib)))
        lst_v[a], lst_v[b] = jnp.where(a_first, va, vb), jnp.where(a_first, vb, va)
        lst_i[a], lst_i[b] = jnp.where(a_first, ia, ib), jnp.where(a_first, ib, ia)

    lists_v, lists_i = [], []
    for s in range(0, n_tiles, g):
        lv, li = val[s:s + g], rid[s:s + g]
        for a, b, ordered in ((0, 1, True), (2, 3, True), (0, 2, True), (1, 3, True), (1, 2, False)):
            exchange(lv, li, a, b, ordered)
        lists_v.append(lv)
        lists_i.append(li)

    vals, ids = [], []
    for k in range(PEER_TOPK):
        if between is not None:
            between(k)
        heads_v = [lv[0] for lv in lists_v]
        heads_i = [li[0] for li in lists_i]
        m = jnp.max(functools.reduce(jnp.maximum, heads_v), axis=0, keepdims=True)
        cand = [jnp.where(hv == m, hi, jnp.inf) for hv, hi in zip(heads_v, heads_i)]
        idx = jnp.min(functools.reduce(jnp.minimum, cand), axis=0, keepdims=True)
        vals.append(m)
        ids.append(idx)
        for lv, li in zip(lists_v, lists_i):
            pop = li[0] == idx
            for d in range(g - 1):
                lv[d] = jnp.where(pop, lv[d + 1], lv[d])
                li[d] = jnp.where(pop, li[d + 1], li[d])
            lv[g - 1] = jnp.where(pop, -jnp.inf, lv[g - 1])
    return jnp.concatenate(vals, axis=0), jnp.concatenate(ids, axis=0).astype(jnp.int32)


def _pick(table, sel):
    out = jnp.zeros_like(table)
    for c in range(PEER_TOPK):
        out = jnp.where(sel == c, table[c:c + 1, :], out)
    return out


def _top16_pairs(s0, s1):
    k = PEER_TOPK
    tokens = s0.shape[1]
    sub = lax.broadcasted_iota(jnp.int32, (SUBLANES, tokens), 0)
    subf = sub.astype(F32)
    lo_v = []
    for d in range(k):
        n_c1 = min(SUBLANES, k // (d + 1))
        v = s0[:SUBLANES, :] + s1[d:d + 1, :]
        lo_v.append(v if n_c1 == SUBLANES else jnp.where(sub < n_c1, v, -jnp.inf))
    lo_i = subf * float(k)
    hi_v = s0[SUBLANES:, :] + s1[0:1, :]
    hi_i = (subf + float(SUBLANES)) * float(k)

    vals, ids = [], []
    for j in range(k):
        m = jnp.max(jnp.maximum(lo_v[0], hi_v), axis=0, keepdims=True)
        cand = jnp.minimum(jnp.where(lo_v[0] == m, lo_i, jnp.inf), jnp.where(hi_v == m, hi_i, jnp.inf))
        idx = jnp.min(cand, axis=0, keepdims=True)
        vals.append(m)
        ids.append(idx)
        pop = lo_i == idx
        for d in range(k - 1 - j):
            lo_v[d] = jnp.where(pop, lo_v[d + 1], lo_v[d])
        lo_i = jnp.where(pop, lo_i + 1.0, lo_i)
        hi_v = jnp.where(hi_i == idx, -jnp.inf, hi_v)
    return jnp.concatenate(vals, axis=0), jnp.concatenate(ids, axis=0).astype(jnp.int32)


def _route_head(qp_ref, sk_ref, h, between):
    tops = []
    for p in range(2):
        hp = 2 * h + p
        q_hp = qp_ref[:, hp * PEER_HALF:(hp + 1) * PEER_HALF]
        s_t = lax.dot_general(sk_ref[hp], q_hp, _NT, preferred_element_type=F32)
        tops.append(_top16_keys(s_t, functools.partial(between, p)))
    (s0, i0), (s1, i1) = tops
    score, ci = _top16_pairs(s0, s1)
    e = jnp.exp(score - jnp.max(score, axis=0, keepdims=True))
    return _pick(i0, ci // PEER_TOPK), _pick(i1, ci % PEER_TOPK), e / jnp.sum(e, axis=0, keepdims=True)


SCATTER_STRIDE = ROUTE_TILE + 1


def _scatter_tokens(tokens, a_scr, b_scr, g_scr, scr):
    nk = PEER_N_KEYS
    key_id = lax.broadcasted_iota(jnp.int32, (nk, N_SEL), 0)
    for t in tokens:
        a_row = a_scr[t:t + 1, :]
        b_row = b_scr[t:t + 1, :]
        g_row = g_scr[t:t + 1, :]
        x = jnp.where(a_row == key_id, g_row, 0.0).astype(BF16)
        y = jnp.where(b_row == key_id, 1.0, 0.0).astype(BF16)
        w_t = lax.dot_general(x, y, _NT, preferred_element_type=F32)
        scr[pl.ds(t, nk, stride=SCATTER_STRIDE), :] = w_t


def _route_scatter_kernel(qp_ref, sk_ref, w_ref, a_scr, b_scr, g_scr, scr):
    @pl.when(pl.program_id(0) == 0)
    def _():
        a_scr[...] = jnp.zeros_like(a_scr)
        b_scr[...] = jnp.zeros_like(b_scr)
        g_scr[...] = jnp.zeros_like(g_scr)

    nk = PEER_N_KEYS
    per_head = ROUTE_TILE // PEER_HEADS
    picked = []
    for h in range(PEER_HEADS):
        def scatter_one(p, k, h=h):
            if k % 2 == 0:
                t = h * per_head + p * (per_head // 2) + k // 2
                _scatter_tokens([t], a_scr, b_scr, g_scr, scr)
        picked.append(_route_head(qp_ref, sk_ref, h, scatter_one))
    for j in range(nk):
        w_ref[:, j * nk:(j + 1) * nk] = scr[pl.ds(j * SCATTER_STRIDE, ROUTE_TILE), :].astype(BF16)
    a_scr[...] = jnp.concatenate([p[0] for p in picked], axis=0).T
    b_scr[...] = jnp.concatenate([p[1] for p in picked], axis=0).T
    g_scr[...] = jnp.concatenate([p[2] for p in picked], axis=0).T


def _route_scatter(qp, subkeys_bf):
    n = qp.shape[0]
    tt = ROUTE_TILE
    n_tiles = n // tt
    return pl.pallas_call(
        _route_scatter_kernel,
        grid=(n_tiles + 1,),
        in_specs=[pl.BlockSpec((tt, PEER_Q_WIDTH), lambda i: (jnp.minimum(i, n_tiles - 1), 0)),
                  pl.BlockSpec(subkeys_bf.shape, lambda i: (0, 0, 0))],
        out_specs=pl.BlockSpec((tt, PEER_N_EXPERTS), lambda i: (jnp.maximum(i - 1, 0), 0)),
        out_shape=jax.ShapeDtypeStruct((n, PEER_N_EXPERTS), BF16),
        scratch_shapes=[pltpu.VMEM((tt, N_SEL), jnp.int32),
                        pltpu.VMEM((tt, N_SEL), jnp.int32),
                        pltpu.VMEM((tt, N_SEL), F32),
                        pltpu.VMEM((PEER_N_KEYS * SCATTER_STRIDE, PEER_N_KEYS), F32)],
        compiler_params=_params("arbitrary"),
    )(qp, subkeys_bf)


EXP_TOKENS = 1024
EXP_BLOCK = 1024
EXP_ROWS = 1024


def _experts_kernel(h_ref, u_ref, v_ref, w_ref, x1_ref, gf_ref, out_ref, acc_ref):
    j = pl.program_id(1)

    @pl.when(j == 0)
    def _():
        acc_ref[...] = jnp.zeros_like(acc_ref)

    for r in range(EXP_TOKENS // EXP_ROWS):
        rows = pl.ds(r * EXP_ROWS, EXP_ROWS)
        act = jax.nn.gelu(lax.dot_general(h_ref[rows, :], u_ref[...].astype(BF16), _NT, preferred_element_type=F32))
        z = (w_ref[rows, :].astype(F32) * act).astype(BF16)
        acc_ref[rows, :] += jnp.dot(z, v_ref[...].astype(BF16), preferred_element_type=F32)

    @pl.when(j == pl.num_programs(1) - 1)
    def _():
        out_ref[...] = _rms(x1_ref[...] + acc_ref[...], gf_ref[...])


def _experts(h2, u_bf, v_bf, w, x1, gf):
    n = h2.shape[0]
    tt, eb = EXP_TOKENS, EXP_BLOCK
    tok = lambda width: pl.BlockSpec((tt, width), lambda i, j: (i, 0))
    return pl.pallas_call(
        _experts_kernel,
        grid=(n // tt, PEER_N_EXPERTS // eb),
        in_specs=[tok(D_MODEL),
                  pl.BlockSpec((eb, D_MODEL), lambda i, j: (j, 0)),
                  pl.BlockSpec((eb, D_MODEL), lambda i, j: (j, 0)),
                  pl.BlockSpec((tt, eb), lambda i, j: (i, j)),
                  tok(D_MODEL),
                  pl.BlockSpec((1, D_MODEL), lambda i, j: (0, 0))],
        out_specs=tok(D_MODEL),
        out_shape=jax.ShapeDtypeStruct((n, D_MODEL), F32),
        scratch_shapes=[pltpu.VMEM((tt, D_MODEL), F32)],
        compiler_params=_params("arbitrary", "arbitrary"),
    )(h2, u_bf, v_bf, w, x1, gf)


def kernel(x, meta, norm1_g, w_in, conv_w, w_branch_a, w_branch_b, w_out, norm2_g, peer_w_q,
           peer_subkeys, peer_u, peer_v, final_g):
    bsz, seq, d = x.shape
    n = bsz * seq
    x2d = x.reshape(n, d)
    g1 = norm1_g.reshape(1, d)
    w_in_bf = w_in.astype(BF16)

    conv3, q, k, v, gates = _in_proj(x2d, g1, w_in_bf, IN_PROJ_TILE)
    conv_meta, _, k_meta, v_meta, _ = _in_proj(meta.astype(x.dtype), g1, w_in_bf, N_META)

    o = _attention(q, k, v, k_meta, v_meta, bsz, seq)

    x1, h2, qp = _mixer(conv3, o, gates, x2d, conv_meta, conv_w,
                        w_branch_a.astype(BF16), w_branch_b.astype(BF16), w_out.astype(BF16),
                        norm2_g.reshape(1, d), peer_w_q.astype(BF16), bsz, seq)

    subkeys_bf = peer_subkeys.reshape(PEER_HEADS * 2, PEER_N_KEYS, PEER_HALF).astype(BF16)
    w = _route_scatter(qp, subkeys_bf)
    out = _experts(h2, peer_u, peer_v, w, x1, final_g.reshape(1, d))
    return out.reshape(bsz, seq, d)
```

```python
import functools

import jax
import jax.numpy as jnp
from jax import lax
from jax.experimental import pallas as pl
from jax.experimental.pallas import tpu as pltpu

D_MODEL = 1024
N_META = 16
CONV_WIDTH = 512
CONV_K = 3
SB_HEADS = 8
SB_HEAD_DIM = 64
SB_WIDTH = SB_HEADS * SB_HEAD_DIM
PEER_HEADS = 8
PEER_N_KEYS = 128
PEER_N_EXPERTS = PEER_N_KEYS * PEER_N_KEYS
PEER_TOPK = 16
PEER_HALF = 128
PEER_Q_WIDTH = PEER_HEADS * 2 * PEER_HALF
N_SEL = PEER_HEADS * PEER_TOPK
GATE_WIDTH = 2 * D_MODEL
EPS = 1e-6
LOG2E = 1.4426950408889634

LANES = 128
SUBLANES = 8
VMEM_LIMIT = 56 * 1024 * 1024

F32 = jnp.float32
BF16 = jnp.bfloat16

_NT = (((1,), (1,)), ((), ()))


def _params(*sem):
    return pltpu.CompilerParams(dimension_semantics=sem, vmem_limit_bytes=VMEM_LIMIT)


def _rms(x, g):
    return x * lax.rsqrt(jnp.mean(x * x, axis=-1, keepdims=True) + EPS) * g


IN_PROJ_TILE = 512

def _in_proj_kernel(x_ref, g_ref, w_ref, conv_ref, q_ref, k_ref, v_ref, gate_ref):
    h = _rms(x_ref[...], g_ref[...]).astype(BF16)
    c0 = 3 * CONV_WIDTH
    conv_ref[...] = jnp.dot(h, w_ref[:, :c0], preferred_element_type=F32)
    scale = SB_HEAD_DIM ** -0.5
    q_ref[...] = (jnp.dot(h, w_ref[:, c0:c0 + SB_WIDTH], preferred_element_type=F32) * scale).astype(BF16)
    k_ref[...] = jnp.dot(h, w_ref[:, c0 + SB_WIDTH:c0 + 2 * SB_WIDTH], preferred_element_type=F32).astype(BF16)
    v_ref[...] = jnp.dot(h, w_ref[:, c0 + 2 * SB_WIDTH:c0 + 3 * SB_WIDTH], preferred_element_type=F32).astype(BF16)
    gate_ref[...] = jnp.dot(h, w_ref[:, c0 + 3 * SB_WIDTH:], preferred_element_type=F32)


def _in_proj(x2d, g, w_in_bf, tm):
    n = x2d.shape[0]
    in_width = w_in_bf.shape[1]
    row = lambda width: pl.BlockSpec((tm, width), lambda i: (i, 0))
    return pl.pallas_call(
        _in_proj_kernel,
        grid=(n // tm,),
        in_specs=[row(D_MODEL),
                  pl.BlockSpec((1, D_MODEL), lambda i: (0, 0)),
                  pl.BlockSpec((D_MODEL, in_width), lambda i: (0, 0))],
        out_specs=[row(3 * CONV_WIDTH), row(SB_WIDTH), row(SB_WIDTH), row(SB_WIDTH), row(GATE_WIDTH)],
        out_shape=[jax.ShapeDtypeStruct((n, 3 * CONV_WIDTH), F32),
                   jax.ShapeDtypeStruct((n, SB_WIDTH), BF16),
                   jax.ShapeDtypeStruct((n, SB_WIDTH), BF16),
                   jax.ShapeDtypeStruct((n, SB_WIDTH), BF16),
                   jax.ShapeDtypeStruct((n, GATE_WIDTH), F32)],
        compiler_params=_params("arbitrary"),
    )(x2d, g, w_in_bf)


ATT_TILE = 256
ATT_HEADS = 8
UNDERFLOW_CSUM = 150.0


def _softplus(z):
    return jnp.maximum(z, 0.0) + jnp.log(1.0 + jnp.exp2(jnp.abs(z) * (-LOG2E)))


def _split_bf16(x):
    hi = x.astype(BF16)
    return hi, (x - hi.astype(F32)).astype(BF16)


def _attn_kernel(q_ref, k_ref, v_ref, km_ref, vm_ref, o_ref, om_scr):
    qi = pl.program_id(2)
    t = ATT_TILE
    heads = range(ATT_HEADS)
    lane = lax.broadcasted_iota(jnp.int32, (1, LANES), 1)
    low_half = lane < SB_HEAD_DIM

    def pair_block(ref_or_val, h):
        p = h // 2
        return ref_or_val[:, p * LANES:(p + 1) * LANES]

    qh = []
    for h in heads:
        q2 = pair_block(q_ref, h)
        qh.append(jnp.where(low_half if h % 2 == 0 else ~low_half, q2, jnp.zeros_like(q2)))

    r = lax.broadcasted_iota(jnp.int32, (t, t), 0)
    c = lax.broadcasted_iota(jnp.int32, (t, t), 1)
    incl = jnp.where(r >= c, 1.0, 0.0).astype(BF16)
    past = c < r
    rm = lax.broadcasted_iota(jnp.int32, (N_META, N_META), 0)
    cm = lax.broadcasted_iota(jnp.int32, (N_META, N_META), 1)
    incl_t = jnp.where(cm >= rm, 1.0, 0.0).astype(BF16)

    def step(start, carry, mask):
        k_t = k_ref[pl.ds(start, t), :]
        v_t = v_ref[pl.ds(start, t), :]
        z = [lax.dot_general(qh[h], pair_block(k_t, h), _NT, preferred_element_type=F32)
             for h in heads]
        sp = [_softplus(z[h]) for h in heads]
        if mask is not None:
            sp = [jnp.where(mask, sp[h], 0.0) for h in heads]
        parts = [_split_bf16(sp[h]) for h in heads]
        suf = [jnp.dot(parts[h][0], incl, preferred_element_type=F32)
               + jnp.dot(parts[h][1], incl, preferred_element_type=F32) for h in heads]
        a = [jnp.exp(z[h] - suf[h] - carry[2 * h + 1]) for h in heads]
        if mask is not None:
            a = [jnp.where(mask, a[h], 0.0) for h in heads]
        out = []
        for h in heads:
            out.append(carry[2 * h] + jnp.dot(a[h].astype(BF16), pair_block(v_t, h), preferred_element_type=F32))
            out.append(carry[2 * h + 1] + suf[h][:, 0:1])
        return tuple(out)

    z_t = [lax.dot_general(pair_block(km_ref, h), qh[h], _NT, preferred_element_type=F32)
           for h in heads]
    parts_t = [_split_bf16(_softplus(z_t[h])) for h in heads]
    suf_t = [jnp.dot(incl_t, parts_t[h][0], preferred_element_type=F32)
             + jnp.dot(incl_t, parts_t[h][1], preferred_element_type=F32) for h in heads]
    a_t = [jnp.exp(z_t[h] - suf_t[h]).astype(BF16) for h in heads]
    for h in heads:
        om_scr[h] = lax.dot_general(a_t[h], pair_block(vm_ref, h), (((0,), (0,)), ((), ())),
                                    preferred_element_type=F32)

    zero = (jnp.zeros((t, LANES), F32), jnp.zeros((t, 1), F32))
    carry = step(pl.multiple_of(qi * t, t), zero * ATT_HEADS, past)

    def min_csum(carry):
        return jnp.min(functools.reduce(jnp.minimum, [carry[2 * h + 1] for h in heads]))

    def more(state):
        it, smallest, _ = state
        return jnp.logical_and(it <= qi, smallest <= UNDERFLOW_CSUM)

    def body(state):
        it, _, carry = state
        carry = step(pl.multiple_of((qi - it) * t, t), carry, None)
        return it + 1, min_csum(carry), carry

    _, _, carry = lax.while_loop(more, body, (jnp.int32(1), min_csum(carry), carry))

    outs = [carry[2 * h] + jnp.exp(-carry[2 * h + 1]) * om_scr[h] for h in heads]
    for p in range(ATT_HEADS // 2):
        o_ref[:, p * LANES:(p + 1) * LANES] = jnp.where(low_half, outs[2 * p], outs[2 * p + 1]).astype(BF16)


def _attention(q, k, v, km, vm, bsz, seq):
    t = ATT_TILE
    width = ATT_HEADS * SB_HEAD_DIM
    n_groups = SB_WIDTH // width
    q3 = q.reshape(bsz, seq, SB_WIDTH)
    k3 = k.reshape(bsz, seq, SB_WIDTH)
    v3 = v.reshape(bsz, seq, SB_WIDTH)
    o = pl.pallas_call(
        _attn_kernel,
        grid=(bsz, n_groups, seq // t),
        in_specs=[pl.BlockSpec((None, t, width), lambda b, p, i: (b, i, p)),
                  pl.BlockSpec((None, seq, width), lambda b, p, i: (b, 0, p)),
                  pl.BlockSpec((None, seq, width), lambda b, p, i: (b, 0, p)),
                  pl.BlockSpec((N_META, width), lambda b, p, i: (0, p)),
                  pl.BlockSpec((N_META, width), lambda b, p, i: (0, p))],
        out_specs=pl.BlockSpec((None, t, width), lambda b, p, i: (b, i, p)),
        out_shape=jax.ShapeDtypeStruct((bsz, seq, SB_WIDTH), BF16),
        scratch_shapes=[pltpu.VMEM((ATT_HEADS, t, LANES), F32)],
        compiler_params=_params("arbitrary", "arbitrary", "arbitrary"),
    )(q3, k3, v3, km, vm)
    return o.reshape(bsz * seq, SB_WIDTH)


MIX_TILE = 512


def _mixer_kernel(conv_ref, o_ref, gate_ref, x_ref, convm_ref, cw_ref, wa_ref, wb_ref, wo_ref,
                  g2_ref, wq_ref, x1_ref, h2_ref, qp_ref, cu_scr):
    ti = pl.program_id(1)
    tl = MIX_TILE
    cw = CONV_WIDTH
    hist = SUBLANES

    @pl.when(ti == 0)
    def _():
        cu_scr[0:hist, :] = (convm_ref[N_META - hist:, cw:2 * cw] * convm_ref[N_META - hist:, 2 * cw:])

    @pl.when(ti != 0)
    def _():
        cu_scr[0:hist, :] = cu_scr[tl:tl + hist, :]

    cu_scr[hist:, :] = conv_ref[:, cw:2 * cw] * conv_ref[:, 2 * cw:]
    conv = (cu_scr[pl.ds(hist - 2, tl), :] * cw_ref[0:1, :]
            + cu_scr[pl.ds(hist - 1, tl), :] * cw_ref[1:2, :]
            + cu_scr[pl.ds(hist, tl), :] * cw_ref[2:3, :])
    y_a = jnp.dot((conv_ref[:, :cw] * conv).astype(BF16), wa_ref[...], preferred_element_type=F32)
    y_b = jnp.dot(o_ref[...], wb_ref[...], preferred_element_type=F32)
    mixed = (jax.nn.sigmoid(gate_ref[:, :D_MODEL]) * y_a
             + jax.nn.sigmoid(gate_ref[:, D_MODEL:]) * y_b)
    x1 = x_ref[...] + jnp.dot(mixed.astype(BF16), wo_ref[...], preferred_element_type=F32)
    x1_ref[...] = x1
    h2 = _rms(x1, g2_ref[...]).astype(BF16)
    h2_ref[...] = h2
    qp_ref[...] = jnp.dot(h2, wq_ref[...], preferred_element_type=F32).astype(BF16)


def _mixer(conv3, o, gates, x2d, conv_meta, conv_w, wa, wb, wo, g2, wq, bsz, seq):
    tl = MIX_TILE
    nt = seq // tl
    n = bsz * seq
    row = lambda width: pl.BlockSpec((tl, width), lambda b, i: (b * nt + i, 0))
    full = lambda a: pl.BlockSpec(a.shape, lambda b, i: (0,) * a.ndim)
    return pl.pallas_call(
        _mixer_kernel,
        grid=(bsz, nt),
        in_specs=[row(3 * CONV_WIDTH), row(SB_WIDTH), row(GATE_WIDTH), row(D_MODEL),
                  full(conv_meta), full(conv_w), full(wa), full(wb), full(wo), full(g2), full(wq)],
        out_specs=[row(D_MODEL), row(D_MODEL), row(PEER_Q_WIDTH)],
        out_shape=[jax.ShapeDtypeStruct((n, D_MODEL), F32),
                   jax.ShapeDtypeStruct((n, D_MODEL), BF16),
                   jax.ShapeDtypeStruct((n, PEER_Q_WIDTH), BF16)],
        scratch_shapes=[pltpu.VMEM((tl + SUBLANES, CONV_WIDTH), F32)],
        compiler_params=_params("arbitrary", "arbitrary"),
    )(conv3, o, gates, x2d, conv_meta, conv_w, wa, wb, wo, g2, wq)


ROUTE_TILE = 128


TOPK_GROUP = 4


def _top16_keys(x, between=None):
    g = TOPK_GROUP
    tokens = x.shape[1]
    n_tiles = PEER_N_KEYS // SUBLANES
    sub = lax.broadcasted_iota(jnp.int32, (SUBLANES, tokens), 0).astype(F32)
    val = [x[k * SUBLANES:(k + 1) * SUBLANES, :] for k in range(n_tiles)]
    rid = [sub + float(k * SUBLANES) for k in range(n_tiles)]

    def exchange(lst_v, lst_i, a, b, ordered_ids):
        va, vb, ia, ib = lst_v[a], lst_v[b], lst_i[a], lst_i[b]
        a_first = (va >= vb) if ordered_ids else ((va > vb) | ((va == vb) & (ia < ib)))
        lst_v[a], lst_v[b] = jnp.where(a_first, va, vb), jnp.where(a_first, vb, va)
        lst_i[a], lst_i[b] = jnp.where(a_first, ia, ib), jnp.where(a_first, ib, ia)

    lists_v, lists_i = [], []
    for s in range(0, n_tiles, g):
        lv, li = val[s:s + g], rid[s:s + g]
        for a, b, ordered in ((0, 1, True), (2, 3, True), (0, 2, True), (1, 3, True), (1, 2, False)):
            exchange(lv, li, a, b, ordered)
        lists_v.append(lv)
        lists_i.append(li)

    vals, ids = [], []
    for k in range(PEER_TOPK):
        if between is not None:
            between(k)
        heads_v = [lv[0] for lv in lists_v]
        heads_i = [li[0] for li in lists_i]
        m = jnp.max(functools.reduce(jnp.maximum, heads_v), axis=0, keepdims=True)
        cand = [jnp.where(hv == m, hi, jnp.inf) for hv, hi in zip(heads_v, heads_i)]
        idx = jnp.min(functools.reduce(jnp.minimum, cand), axis=0, keepdims=True)
        vals.append(m)
        ids.append(idx)
        for lv, li in zip(lists_v, lists_i):
            pop = li[0] == idx
            for d in range(g - 1):
                lv[d] = jnp.where(pop, lv[d + 1], lv[d])
                li[d] = jnp.where(pop, li[d + 1], li[d])
            lv[g - 1] = jnp.where(pop, -jnp.inf, lv[g - 1])
    return jnp.concatenate(vals, axis=0), jnp.concatenate(ids, axis=0).astype(jnp.int32)


def _pick(table, sel):
    out = jnp.zeros_like(table)
    for c in range(PEER_TOPK):
        out = jnp.where(sel == c, table[c:c + 1, :], out)
    return out


def _top16_pairs(s0, s1):
    k = PEER_TOPK
    tokens = s0.shape[1]
    sub = lax.broadcasted_iota(jnp.int32, (SUBLANES, tokens), 0)
    subf = sub.astype(F32)
    lo_v = []
    for d in range(k):
        n_c1 = min(SUBLANES, k // (d + 1))
        v = s0[:SUBLANES, :] + s1[d:d + 1, :]
        lo_v.append(v if n_c1 == SUBLANES else jnp.where(sub < n_c1, v, -jnp.inf))
    lo_i = subf * float(k)
    hi_v = s0[SUBLANES:, :] + s1[0:1, :]
    hi_i = (subf + float(SUBLANES)) * float(k)

    vals, ids = [], []
    for j in range(k):
        m = jnp.max(jnp.maximum(lo_v[0], hi_v), axis=0, keepdims=True)
        cand = jnp.minimum(jnp.where(lo_v[0] == m, lo_i, jnp.inf), jnp.where(hi_v == m, hi_i, jnp.inf))
        idx = jnp.min(cand, axis=0, keepdims=True)
        vals.append(m)
        ids.append(idx)
        pop = lo_i == idx
        for d in range(k - 1 - j):
            lo_v[d] = jnp.where(pop, lo_v[d + 1], lo_v[d])
        lo_i = jnp.where(pop, lo_i + 1.0, lo_i)
        hi_v = jnp.where(hi_i == idx, -jnp.inf, hi_v)
    return jnp.concatenate(vals, axis=0), jnp.concatenate(ids, axis=0).astype(jnp.int32)


def _route_head(qp_ref, sk_ref, h, between):
    tops = []
    for p in range(2):
        hp = 2 * h + p
        q_hp = qp_ref[:, hp * PEER_HALF:(hp + 1) * PEER_HALF]
        s_t = lax.dot_general(sk_ref[hp], q_hp, _NT, preferred_element_type=F32)
        tops.append(_top16_keys(s_t, functools.partial(between, p)))
    (s0, i0), (s1, i1) = tops
    score, ci = _top16_pairs(s0, s1)
    e = jnp.exp(score - jnp.max(score, axis=0, keepdims=True))
    return _pick(i0, ci // PEER_TOPK), _pick(i1, ci % PEER_TOPK), e / jnp.sum(e, axis=0, keepdims=True)


SCATTER_STRIDE = ROUTE_TILE + 1


def _scatter_tokens(tokens, a_scr, b_scr, g_scr, scr):
    nk = PEER_N_KEYS
    key_id = lax.broadcasted_iota(jnp.int32, (nk, N_SEL), 0)
    for t in tokens:
        a_row = a_scr[t:t + 1, :]
        b_row = b_scr[t:t + 1, :]
        g_row = g_scr[t:t + 1, :]
        x = jnp.where(a_row == key_id, g_row, 0.0).astype(BF16)
        y = jnp.where(b_row == key_id, 1.0, 0.0).astype(BF16)
        w_t = lax.dot_general(x, y, _NT, preferred_element_type=F32)
        scr[pl.ds(t, nk, stride=SCATTER_STRIDE), :] = w_t


def _route_scatter_kernel(qp_ref, sk_ref, w_ref, a_scr, b_scr, g_scr, scr):
    @pl.when(pl.program_id(0) == 0)
    def _():
        a_scr[...] = jnp.zeros_like(a_scr)
        b_scr[...] = jnp.zeros_like(b_scr)
        g_scr[...] = jnp.zeros_like(g_scr)

    nk = PEER_N_KEYS
    per_head = ROUTE_TILE // PEER_HEADS
    picked = []
    for h in range(PEER_HEADS):
        def scatter_one(p, k, h=h):
            if k % 2 == 0:
                t = h * per_head + p * (per_head // 2) + k // 2
                _scatter_tokens([t], a_scr, b_scr, g_scr, scr)
        picked.append(_route_head(qp_ref, sk_ref, h, scatter_one))
    for j in range(nk):
        w_ref[:, j * nk:(j + 1) * nk] = scr[pl.ds(j * SCATTER_STRIDE, ROUTE_TILE), :].astype(BF16)
    a_scr[...] = jnp.concatenate([p[0] for p in picked], axis=0).T
    b_scr[...] = jnp.concatenate([p[1] for p in picked], axis=0).T
    g_scr[...] = jnp.concatenate([p[2] for p in picked], axis=0).T


def _route_scatter(qp, subkeys_bf):
    n = qp.shape[0]
    tt = ROUTE_TILE
    n_tiles = n // tt
    return pl.pallas_call(
        _route_scatter_kernel,
        grid=(n_tiles + 1,),
        in_specs=[pl.BlockSpec((tt, PEER_Q_WIDTH), lambda i: (jnp.minimum(i, n_tiles - 1), 0)),
                  pl.BlockSpec(subkeys_bf.shape, lambda i: (0, 0, 0))],
        out_specs=pl.BlockSpec((tt, PEER_N_EXPERTS), lambda i: (jnp.maximum(i - 1, 0), 0)),
        out_shape=jax.ShapeDtypeStruct((n, PEER_N_EXPERTS), BF16),
        scratch_shapes=[pltpu.VMEM((tt, N_SEL), jnp.int32),
                        pltpu.VMEM((tt, N_SEL), jnp.int32),
                        pltpu.VMEM((tt, N_SEL), F32),
                        pltpu.VMEM((PEER_N_KEYS * SCATTER_STRIDE, PEER_N_KEYS), F32)],
        compiler_params=_params("arbitrary"),
    )(qp, subkeys_bf)


EXP_TOKENS = 1024
EXP_BLOCK = 1024


def _experts_kernel(h_ref, u_ref, v_ref, w_ref, x1_ref, gf_ref, out_ref, acc_ref):
    j = pl.program_id(1)

    @pl.when(j == 0)
    def _():
        acc_ref[...] = jnp.zeros_like(acc_ref)

    act = jax.nn.gelu(lax.dot_general(h_ref[...], u_ref[...].astype(BF16), _NT, preferred_element_type=F32))
    z = (w_ref[...].astype(F32) * act).astype(BF16)
    acc_ref[...] += jnp.dot(z, v_ref[...].astype(BF16), preferred_element_type=F32)

    @pl.when(j == pl.num_programs(1) - 1)
    def _():
        out_ref[...] = _rms(x1_ref[...] + acc_ref[...], gf_ref[...])


def _experts(h2, u_tab, v_tab, w, x1, gf):
    n = h2.shape[0]
    tt, eb = EXP_TOKENS, EXP_BLOCK
    tok = lambda width: pl.BlockSpec((tt, width), lambda i, j: (i, 0))
    return pl.pallas_call(
        _experts_kernel,
        grid=(n // tt, PEER_N_EXPERTS // eb),
        in_specs=[tok(D_MODEL),
                  pl.BlockSpec((eb, D_MODEL), lambda i, j: (j, 0)),
                  pl.BlockSpec((eb, D_MODEL), lambda i, j: (j, 0)),
                  pl.BlockSpec((tt, eb), lambda i, j: (i, j)),
                  tok(D_MODEL),
                  pl.BlockSpec((1, D_MODEL), lambda i, j: (0, 0))],
        out_specs=tok(D_MODEL),
        out_shape=jax.ShapeDtypeStruct((n, D_MODEL), F32),
        scratch_shapes=[pltpu.VMEM((tt, D_MODEL), F32)],
        compiler_params=_params("arbitrary", "arbitrary"),
    )(h2, u_tab, v_tab, w, x1, gf)


def kernel(x, meta, norm1_g, w_in, conv_w, w_branch_a, w_branch_b, w_out, norm2_g, peer_w_q,
           peer_subkeys, peer_u, peer_v, final_g):
    bsz, seq, d = x.shape
    n = bsz * seq
    x2d = x.reshape(n, d)
    g1 = norm1_g.reshape(1, d)
    w_in_bf = w_in.astype(BF16)

    conv3, q, k, v, gates = _in_proj(x2d, g1, w_in_bf, IN_PROJ_TILE)
    conv_meta, _, k_meta, v_meta, _ = _in_proj(meta.astype(x.dtype), g1, w_in_bf, N_META)

    o = _attention(q, k, v, k_meta, v_meta, bsz, seq)

    x1, h2, qp = _mixer(conv3, o, gates, x2d, conv_meta, conv_w,
                        w_branch_a.astype(BF16), w_branch_b.astype(BF16), w_out.astype(BF16),
                        norm2_g.reshape(1, d), peer_w_q.astype(BF16), bsz, seq)

    subkeys_bf = peer_subkeys.reshape(PEER_HEADS * 2, PEER_N_KEYS, PEER_HALF).astype(BF16)
    w = _route_scatter(qp, subkeys_bf)
    out = _experts(h2, peer_u, peer_v, w, x1, final_g.reshape(1, d))
    return out.reshape(bsz, seq, d)
```

```python
import functools

import jax
import jax.numpy as jnp
from jax import lax
from jax.experimental import pallas as pl
from jax.experimental.pallas import tpu as pltpu

D_MODEL = 1024
N_META = 16
CONV_WIDTH = 512
CONV_K = 3
SB_HEADS = 8
SB_HEAD_DIM = 64
SB_WIDTH = SB_HEADS * SB_HEAD_DIM
PEER_HEADS = 8
PEER_N_KEYS = 128
PEER_N_EXPERTS = PEER_N_KEYS * PEER_N_KEYS
PEER_TOPK = 16
PEER_HALF = 128
PEER_Q_WIDTH = PEER_HEADS * 2 * PEER_HALF
N_SEL = PEER_HEADS * PEER_TOPK
GATE_WIDTH = 2 * D_MODEL
EPS = 1e-6
LOG2E = 1.4426950408889634

LANES = 128
SUBLANES = 8
VMEM_LIMIT = 56 * 1024 * 1024

F32 = jnp.float32
BF16 = jnp.bfloat16

_NT = (((1,), (1,)), ((), ()))


def _params(*sem):
    return pltpu.CompilerParams(dimension_semantics=sem, vmem_limit_bytes=VMEM_LIMIT)


def _rms(x, g):
    return x * lax.rsqrt(jnp.mean(x * x, axis=-1, keepdims=True) + EPS) * g


IN_PROJ_TILE = 512

def _in_proj_kernel(x_ref, g_ref, w_ref, conv_ref, q_ref, k_ref, v_ref, gate_ref):
    h = _rms(x_ref[...], g_ref[...]).astype(BF16)
    c0 = 3 * CONV_WIDTH
    conv_ref[...] = jnp.dot(h, w_ref[:, :c0], preferred_element_type=F32)
    scale = SB_HEAD_DIM ** -0.5
    q_ref[...] = (jnp.dot(h, w_ref[:, c0:c0 + SB_WIDTH], preferred_element_type=F32) * scale).astype(BF16)
    k_ref[...] = jnp.dot(h, w_ref[:, c0 + SB_WIDTH:c0 + 2 * SB_WIDTH], preferred_element_type=F32).astype(BF16)
    v_ref[...] = jnp.dot(h, w_ref[:, c0 + 2 * SB_WIDTH:c0 + 3 * SB_WIDTH], preferred_element_type=F32).astype(BF16)
    gate_ref[...] = jnp.dot(h, w_ref[:, c0 + 3 * SB_WIDTH:], preferred_element_type=F32)


def _in_proj(x2d, g, w_in_bf, tm):
    n = x2d.shape[0]
    in_width = w_in_bf.shape[1]
    row = lambda width: pl.BlockSpec((tm, width), lambda i: (i, 0))
    return pl.pallas_call(
        _in_proj_kernel,
        grid=(n // tm,),
        in_specs=[row(D_MODEL),
                  pl.BlockSpec((1, D_MODEL), lambda i: (0, 0)),
                  pl.BlockSpec((D_MODEL, in_width), lambda i: (0, 0))],
        out_specs=[row(3 * CONV_WIDTH), row(SB_WIDTH), row(SB_WIDTH), row(SB_WIDTH), row(GATE_WIDTH)],
        out_shape=[jax.ShapeDtypeStruct((n, 3 * CONV_WIDTH), F32),
                   jax.ShapeDtypeStruct((n, SB_WIDTH), BF16),
                   jax.ShapeDtypeStruct((n, SB_WIDTH), BF16),
                   jax.ShapeDtypeStruct((n, SB_WIDTH), BF16),
                   jax.ShapeDtypeStruct((n, GATE_WIDTH), F32)],
        compiler_params=_params("arbitrary"),
    )(x2d, g, w_in_bf)


ATT_TILE = 256
ATT_HEADS = 8
UNDERFLOW_CSUM = 150.0


def _softplus(z):
    return jnp.maximum(z, 0.0) + jnp.log(1.0 + jnp.exp2(jnp.abs(z) * (-LOG2E)))


def _split_bf16(x):
    hi = x.astype(BF16)
    return hi, (x - hi.astype(F32)).astype(BF16)


def _attn_kernel(q_ref, k_ref, v_ref, km_ref, vm_ref, o_ref):
    qi = pl.program_id(2)
    t = ATT_TILE
    heads = range(ATT_HEADS)
    lane = lax.broadcasted_iota(jnp.int32, (1, LANES), 1)
    low_half = lane < SB_HEAD_DIM

    def pair_block(ref_or_val, h):
        p = h // 2
        return ref_or_val[:, p * LANES:(p + 1) * LANES]

    qh = []
    for h in heads:
        q2 = pair_block(q_ref, h)
        qh.append(jnp.where(low_half if h % 2 == 0 else ~low_half, q2, jnp.zeros_like(q2)))

    r = lax.broadcasted_iota(jnp.int32, (t, t), 0)
    c = lax.broadcasted_iota(jnp.int32, (t, t), 1)
    incl = jnp.where(r >= c, 1.0, 0.0).astype(BF16)
    past = c < r
    rm = lax.broadcasted_iota(jnp.int32, (N_META, N_META), 0)
    cm = lax.broadcasted_iota(jnp.int32, (N_META, N_META), 1)
    incl_t = jnp.where(cm >= rm, 1.0, 0.0).astype(BF16)

    def step(start, carry, mask):
        k_t = k_ref[pl.ds(start, t), :]
        v_t = v_ref[pl.ds(start, t), :]
        z = [lax.dot_general(qh[h], pair_block(k_t, h), _NT, preferred_element_type=F32)
             for h in heads]
        sp = [_softplus(z[h]) for h in heads]
        if mask is not None:
            sp = [jnp.where(mask, sp[h], 0.0) for h in heads]
        parts = [_split_bf16(sp[h]) for h in heads]
        suf = [jnp.dot(parts[h][0], incl, preferred_element_type=F32)
               + jnp.dot(parts[h][1], incl, preferred_element_type=F32) for h in heads]
        a = [jnp.exp(z[h] - suf[h] - carry[2 * h + 1]) for h in heads]
        if mask is not None:
            a = [jnp.where(mask, a[h], 0.0) for h in heads]
        out = []
        for h in heads:
            out.append(carry[2 * h] + jnp.dot(a[h].astype(BF16), pair_block(v_t, h), preferred_element_type=F32))
            out.append(carry[2 * h + 1] + suf[h][:, 0:1])
        return tuple(out)

    zero = (jnp.zeros((t, LANES), F32), jnp.zeros((t, 1), F32))
    carry = step(pl.multiple_of(qi * t, t), zero * ATT_HEADS, past)

    def min_csum(carry):
        return jnp.min(functools.reduce(jnp.minimum, [carry[2 * h + 1] for h in heads]))

    def more(state):
        it, smallest, _ = state
        return jnp.logical_and(it <= qi, smallest <= UNDERFLOW_CSUM)

    def body(state):
        it, _, carry = state
        carry = step(pl.multiple_of((qi - it) * t, t), carry, None)
        return it + 1, min_csum(carry), carry

    _, smallest, carry = lax.while_loop(more, body, (jnp.int32(1), min_csum(carry), carry))

    def write(outs):
        for p in range(ATT_HEADS // 2):
            o_ref[:, p * LANES:(p + 1) * LANES] = jnp.where(low_half, outs[2 * p], outs[2 * p + 1]).astype(BF16)

    @pl.when(smallest > UNDERFLOW_CSUM)
    def _():
        write([carry[2 * h] for h in heads])

    @pl.when(smallest <= UNDERFLOW_CSUM)
    def _():
        z_t = [lax.dot_general(pair_block(km_ref, h), qh[h], _NT, preferred_element_type=F32)
               for h in heads]
        parts_t = [_split_bf16(_softplus(z_t[h])) for h in heads]
        suf_t = [jnp.dot(incl_t, parts_t[h][0], preferred_element_type=F32)
                 + jnp.dot(incl_t, parts_t[h][1], preferred_element_type=F32) for h in heads]
        a_t = [jnp.exp(z_t[h] - suf_t[h]).astype(BF16) for h in heads]
        o_meta = [lax.dot_general(a_t[h], pair_block(vm_ref, h), (((0,), (0,)), ((), ())),
                                  preferred_element_type=F32) for h in heads]
        write([carry[2 * h] + jnp.exp(-carry[2 * h + 1]) * o_meta[h] for h in heads])


def _attention(q, k, v, km, vm, bsz, seq):
    t = ATT_TILE
    width = ATT_HEADS * SB_HEAD_DIM
    n_groups = SB_WIDTH // width
    q3 = q.reshape(bsz, seq, SB_WIDTH)
    k3 = k.reshape(bsz, seq, SB_WIDTH)
    v3 = v.reshape(bsz, seq, SB_WIDTH)
    o = pl.pallas_call(
        _attn_kernel,
        grid=(bsz, n_groups, seq // t),
        in_specs=[pl.BlockSpec((None, t, width), lambda b, p, i: (b, i, p)),
                  pl.BlockSpec((None, seq, width), lambda b, p, i: (b, 0, p)),
                  pl.BlockSpec((None, seq, width), lambda b, p, i: (b, 0, p)),
                  pl.BlockSpec((N_META, width), lambda b, p, i: (0, p)),
                  pl.BlockSpec((N_META, width), lambda b, p, i: (0, p))],
        out_specs=pl.BlockSpec((None, t, width), lambda b, p, i: (b, i, p)),
        out_shape=jax.ShapeDtypeStruct((bsz, seq, SB_WIDTH), BF16),
        compiler_params=_params("arbitrary", "arbitrary", "arbitrary"),
    )(q3, k3, v3, km, vm)
    return o.reshape(bsz * seq, SB_WIDTH)


MIX_TILE = 512


def _mixer_kernel(conv_ref, o_ref, gate_ref, x_ref, convm_ref, cw_ref, wa_ref, wb_ref, wo_ref,
                  g2_ref, wq_ref, x1_ref, h2_ref, qp_ref, cu_scr):
    ti = pl.program_id(1)
    tl = MIX_TILE
    cw = CONV_WIDTH
    hist = SUBLANES

    @pl.when(ti == 0)
    def _():
        cu_scr[0:hist, :] = (convm_ref[N_META - hist:, cw:2 * cw] * convm_ref[N_META - hist:, 2 * cw:])

    @pl.when(ti != 0)
    def _():
        cu_scr[0:hist, :] = cu_scr[tl:tl + hist, :]

    cu_scr[hist:, :] = conv_ref[:, cw:2 * cw] * conv_ref[:, 2 * cw:]
    conv = (cu_scr[pl.ds(hist - 2, tl), :] * cw_ref[0:1, :]
            + cu_scr[pl.ds(hist - 1, tl), :] * cw_ref[1:2, :]
            + cu_scr[pl.ds(hist, tl), :] * cw_ref[2:3, :])
    y_a = jnp.dot((conv_ref[:, :cw] * conv).astype(BF16), wa_ref[...], preferred_element_type=F32)
    y_b = jnp.dot(o_ref[...], wb_ref[...], preferred_element_type=F32)
    mixed = (jax.nn.sigmoid(gate_ref[:, :D_MODEL]) * y_a
             + jax.nn.sigmoid(gate_ref[:, D_MODEL:]) * y_b)
    x1 = x_ref[...] + jnp.dot(mixed.astype(BF16), wo_ref[...], preferred_element_type=F32)
    x1_ref[...] = x1
    h2 = _rms(x1, g2_ref[...]).astype(BF16)
    h2_ref[...] = h2
    qp_ref[...] = jnp.dot(h2, wq_ref[...], preferred_element_type=F32).astype(BF16)


def _mixer(conv3, o, gates, x2d, conv_meta, conv_w, wa, wb, wo, g2, wq, bsz, seq):
    tl = MIX_TILE
    nt = seq // tl
    n = bsz * seq
    row = lambda width: pl.BlockSpec((tl, width), lambda b, i: (b * nt + i, 0))
    full = lambda a: pl.BlockSpec(a.shape, lambda b, i: (0,) * a.ndim)
    return pl.pallas_call(
        _mixer_kernel,
        grid=(bsz, nt),
        in_specs=[row(3 * CONV_WIDTH), row(SB_WIDTH), row(GATE_WIDTH), row(D_MODEL),
                  full(conv_meta), full(conv_w), full(wa), full(wb), full(wo), full(g2), full(wq)],
        out_specs=[row(D_MODEL), row(D_MODEL), row(PEER_Q_WIDTH)],
        out_shape=[jax.ShapeDtypeStruct((n, D_MODEL), F32),
                   jax.ShapeDtypeStruct((n, D_MODEL), BF16),
                   jax.ShapeDtypeStruct((n, PEER_Q_WIDTH), BF16)],
        scratch_shapes=[pltpu.VMEM((tl + SUBLANES, CONV_WIDTH), F32)],
        compiler_params=_params("arbitrary", "arbitrary"),
    )(conv3, o, gates, x2d, conv_meta, conv_w, wa, wb, wo, g2, wq)


ROUTE_TILE = 128


TOPK_GROUP = 4


def _top16_keys(x, between=None):
    g = TOPK_GROUP
    tokens = x.shape[1]
    n_tiles = PEER_N_KEYS // SUBLANES
    sub = lax.broadcasted_iota(jnp.int32, (SUBLANES, tokens), 0).astype(F32)
    val = [x[k * SUBLANES:(k + 1) * SUBLANES, :] for k in range(n_tiles)]
    rid = [sub + float(k * SUBLANES) for k in range(n_tiles)]

    def exchange(lst_v, lst_i, a, b, ordered_ids):
        va, vb, ia, ib = lst_v[a], lst_v[b], lst_i[a], lst_i[b]
        a_first = (va >= vb) if ordered_ids else ((va > vb) | ((va == vb) & (ia < ib)))
        lst_v[a], lst_v[b] = jnp.where(a_first, va, vb), jnp.where(a_first, vb, va)
        lst_i[a], lst_i[b] = jnp.where(a_first, ia, ib), jnp.where(a_first, ib, ia)

    lists_v, lists_i = [], []
    for s in range(0, n_tiles, g):
        lv, li = val[s:s + g], rid[s:s + g]
        for a, b, ordered in ((0, 1, True), (2, 3, True), (0, 2, True), (1, 3, True), (1, 2, False)):
            exchange(lv, li, a, b, ordered)
        lists_v.append(lv)
        lists_i.append(li)

    vals, ids = [], []
    for k in range(PEER_TOPK):
        if between is not None:
            between(k)
        heads_v = [lv[0] for lv in lists_v]
        heads_i = [li[0] for li in lists_i]
        m = jnp.max(functools.reduce(jnp.maximum, heads_v), axis=0, keepdims=True)
        cand = [jnp.where(hv == m, hi, jnp.inf) for hv, hi in zip(heads_v, heads_i)]
        idx = jnp.min(functools.reduce(jnp.minimum, cand), axis=0, keepdims=True)
        vals.append(m)
        ids.append(idx)
        for lv, li in zip(lists_v, lists_i):
            pop = li[0] == idx
            for d in range(g - 1):
                lv[d] = jnp.where(pop, lv[d + 1], lv[d])
                li[d] = jnp.where(pop, li[d + 1], li[d])
            lv[g - 1] = jnp.where(pop, -jnp.inf, lv[g - 1])
    return jnp.concatenate(vals, axis=0), jnp.concatenate(ids, axis=0).astype(jnp.int32)


def _pick(table, sel):
    out = jnp.zeros_like(table)
    for c in range(PEER_TOPK):
        out = jnp.where(sel == c, table[c:c + 1, :], out)
    return out


def _top16_pairs(s0, s1):
    k = PEER_TOPK
    tokens = s0.shape[1]
    sub = lax.broadcasted_iota(jnp.int32, (SUBLANES, tokens), 0)
    subf = sub.astype(F32)
    lo_v = []
    for d in range(k):
        n_c1 = min(SUBLANES, k // (d + 1))
        v = s0[:SUBLANES, :] + s1[d:d + 1, :]
        lo_v.append(v if n_c1 == SUBLANES else jnp.where(sub < n_c1, v, -jnp.inf))
    lo_i = subf * float(k)
    hi_v = s0[SUBLANES:, :] + s1[0:1, :]
    hi_i = (subf + float(SUBLANES)) * float(k)

    vals, ids = [], []
    for j in range(k):
        m = jnp.max(jnp.maximum(lo_v[0], hi_v), axis=0, keepdims=True)
        cand = jnp.minimum(jnp.where(lo_v[0] == m, lo_i, jnp.inf), jnp.where(hi_v == m, hi_i, jnp.inf))
        idx = jnp.min(cand, axis=0, keepdims=True)
        vals.append(m)
        ids.append(idx)
        pop = lo_i == idx
        for d in range(k - 1 - j):
            lo_v[d] = jnp.where(pop, lo_v[d + 1], lo_v[d])
        lo_i = jnp.where(pop, lo_i + 1.0, lo_i)
        hi_v = jnp.where(hi_i == idx, -jnp.inf, hi_v)
    return jnp.concatenate(vals, axis=0), jnp.concatenate(ids, axis=0).astype(jnp.int32)


def _route_head(qp_ref, sk_ref, h, between):
    tops = []
    for p in range(2):
        hp = 2 * h + p
        q_hp = qp_ref[:, hp * PEER_HALF:(hp + 1) * PEER_HALF]
        s_t = lax.dot_general(sk_ref[hp], q_hp, _NT, preferred_element_type=F32)
        tops.append(_top16_keys(s_t, functools.partial(between, p)))
    (s0, i0), (s1, i1) = tops
    score, ci = _top16_pairs(s0, s1)
    e = jnp.exp(score - jnp.max(score, axis=0, keepdims=True))
    return _pick(i0, ci // PEER_TOPK), _pick(i1, ci % PEER_TOPK), e / jnp.sum(e, axis=0, keepdims=True)


SCATTER_STRIDE = ROUTE_TILE + 1


def _scatter_tokens(tokens, a_scr, b_scr, g_scr, scr):
    nk = PEER_N_KEYS
    key_id = lax.broadcasted_iota(jnp.int32, (nk, N_SEL), 0)
    for t in tokens:
        a_row = a_scr[t:t + 1, :]
        b_row = b_scr[t:t + 1, :]
        g_row = g_scr[t:t + 1, :]
        x = jnp.where(a_row == key_id, g_row, 0.0).astype(BF16)
        y = jnp.where(b_row == key_id, 1.0, 0.0).astype(BF16)
        w_t = lax.dot_general(x, y, _NT, preferred_element_type=F32)
        scr[pl.ds(t, nk, stride=SCATTER_STRIDE), :] = w_t


def _route_scatter_kernel(qp_ref, sk_ref, w_ref, a_scr, b_scr, g_scr, scr):
    @pl.when(pl.program_id(0) == 0)
    def _():
        a_scr[...] = jnp.zeros_like(a_scr)
        b_scr[...] = jnp.zeros_like(b_scr)
        g_scr[...] = jnp.zeros_like(g_scr)

    nk = PEER_N_KEYS
    per_head = ROUTE_TILE // PEER_HEADS
    picked = []
    for h in range(PEER_HEADS):
        def scatter_one(p, k, h=h):
            if k % 2 == 0:
                t = h * per_head + p * (per_head // 2) + k // 2
                _scatter_tokens([t], a_scr, b_scr, g_scr, scr)
        picked.append(_route_head(qp_ref, sk_ref, h, scatter_one))
    for j in range(nk):
        w_ref[:, j * nk:(j + 1) * nk] = scr[pl.ds(j * SCATTER_STRIDE, ROUTE_TILE), :].astype(BF16)
    a_scr[...] = jnp.concatenate([p[0] for p in picked], axis=0).T
    b_scr[...] = jnp.concatenate([p[1] for p in picked], axis=0).T
    g_scr[...] = jnp.concatenate([p[2] for p in picked], axis=0).T


def _route_scatter(qp, subkeys_bf):
    n = qp.shape[0]
    tt = ROUTE_TILE
    n_tiles = n // tt
    return pl.pallas_call(
        _route_scatter_kernel,
        grid=(n_tiles + 1,),
        in_specs=[pl.BlockSpec((tt, PEER_Q_WIDTH), lambda i: (jnp.minimum(i, n_tiles - 1), 0)),
                  pl.BlockSpec(subkeys_bf.shape, lambda i: (0, 0, 0))],
        out_specs=pl.BlockSpec((tt, PEER_N_EXPERTS), lambda i: (jnp.maximum(i - 1, 0), 0)),
        out_shape=jax.ShapeDtypeStruct((n, PEER_N_EXPERTS), BF16),
        scratch_shapes=[pltpu.VMEM((tt, N_SEL), jnp.int32),
                        pltpu.VMEM((tt, N_SEL), jnp.int32),
                        pltpu.VMEM((tt, N_SEL), F32),
                        pltpu.VMEM((PEER_N_KEYS * SCATTER_STRIDE, PEER_N_KEYS), F32)],
        compiler_params=_params("arbitrary"),
    )(qp, subkeys_bf)


EXP_TOKENS = 1024
EXP_BLOCK = 1024


def _experts_kernel(h_ref, u_ref, v_ref, w_ref, x1_ref, gf_ref, out_ref, acc_ref):
    j = pl.program_id(1)

    @pl.when(j == 0)
    def _():
        acc_ref[...] = jnp.zeros_like(acc_ref)

    act = jax.nn.gelu(lax.dot_general(h_ref[...], u_ref[...].astype(BF16), _NT, preferred_element_type=F32))
    z = (w_ref[...].astype(F32) * act).astype(BF16)
    acc_ref[...] += jnp.dot(z, v_ref[...].astype(BF16), preferred_element_type=F32)

    @pl.when(j == pl.num_programs(1) - 1)
    def _():
        out_ref[...] = _rms(x1_ref[...] + acc_ref[...], gf_ref[...])


def _experts(h2, u_tab, v_tab, w, x1, gf):
    n = h2.shape[0]
    tt, eb = EXP_TOKENS, EXP_BLOCK
    tok = lambda width: pl.BlockSpec((tt, width), lambda i, j: (i, 0))
    return pl.pallas_call(
        _experts_kernel,
        grid=(n // tt, PEER_N_EXPERTS // eb),
        in_specs=[tok(D_MODEL),
                  pl.BlockSpec((eb, D_MODEL), lambda i, j: (j, 0)),
                  pl.BlockSpec((eb, D_MODEL), lambda i, j: (j, 0)),
                  pl.BlockSpec((tt, eb), lambda i, j: (i, j)),
                  tok(D_MODEL),
                  pl.BlockSpec((1, D_MODEL), lambda i, j: (0, 0))],
        out_specs=tok(D_MODEL),
        out_shape=jax.ShapeDtypeStruct((n, D_MODEL), F32),
        scratch_shapes=[pltpu.VMEM((tt, D_MODEL), F32)],
        compiler_params=_params("arbitrary", "arbitrary"),
    )(h2, u_tab, v_tab, w, x1, gf)


def kernel(x, meta, norm1_g, w_in, conv_w, w_branch_a, w_branch_b, w_out, norm2_g, peer_w_q,
           peer_subkeys, peer_u, peer_v, final_g):
    bsz, seq, d = x.shape
    n = bsz * seq
    x2d = x.reshape(n, d)
    g1 = norm1_g.reshape(1, d)
    w_in_bf = w_in.astype(BF16)

    conv3, q, k, v, gates = _in_proj(x2d, g1, w_in_bf, IN_PROJ_TILE)
    conv_meta, _, k_meta, v_meta, _ = _in_proj(meta.astype(x.dtype), g1, w_in_bf, N_META)

    o = _attention(q, k, v, k_meta, v_meta, bsz, seq)

    x1, h2, qp = _mixer(conv3, o, gates, x2d, conv_meta, conv_w,
                        w_branch_a.astype(BF16), w_branch_b.astype(BF16), w_out.astype(BF16),
                        norm2_g.reshape(1, d), peer_w_q.astype(BF16), bsz, seq)

    subkeys_bf = peer_subkeys.reshape(PEER_HEADS * 2, PEER_N_KEYS, PEER_HALF).astype(BF16)
    w = _route_scatter(qp, subkeys_bf)
    out = _experts(h2, peer_u, peer_v, w, x1, final_g.reshape(1, d))
    return out.reshape(bsz, seq, d)
```
